```python
import jax, jax.numpy as jnp
from jax import lax
import numpy as np

D_MODEL = 1024
BATCH = 8
SEQ = 8192
DEPTH = 1
DEC_BATCH = 16
DEC_SEQ = 2048
PAST_LEN = 128

MLA_HEADS = 8
QK_NOPE = 64
QK_ROPE = 32
V_HEAD = 64
Q_LORA = 384
KV_LORA = 256
MLA_WIDTH = MLA_HEADS * V_HEAD
ROPE_BASE = 10000.0
Q_BLOCK = 128
RWKV_HEADS = 8
RWKV_HEAD = 64
RWKV_WIDTH = RWKV_HEADS * RWKV_HEAD
DECAY_LORA = 64
ICLR_LORA = 64
GATE_LORA = 128
N_BRANCH = 2
D_FF = ((8 * D_MODEL // 3 + 255) // 256) * 256
EPS = 1e-6
LNX_EPS = 64e-5
MLA_IN = Q_LORA + KV_LORA + QK_ROPE
RWKV_IN = 3 * RWKV_WIDTH + DECAY_LORA + ICLR_LORA + GATE_LORA
GATE_IN = N_BRANCH * D_MODEL
IN_COLS = MLA_IN + RWKV_IN + GATE_IN

kernel_name = "hybrid_mla_rwkv7_adaln_encoder"


def rmsnorm(x, g):
    xf = x.astype(jnp.float32)
    y = xf * lax.rsqrt(jnp.mean(xf * xf, axis=-1, keepdims=True) + EPS)
    return (y * g.astype(jnp.float32)).astype(x.dtype)


def rope_tables(T):
    half = QK_ROPE // 2
    inv = ROPE_BASE ** (-jnp.arange(half, dtype=jnp.float32) / half)
    ang = jnp.arange(T, dtype=jnp.float32)[:, None] * inv[None, :]
    return jnp.cos(ang), jnp.sin(ang)


def apply_rope(x, cos, sin):
    xf = x.astype(jnp.float32)
    x1, x2 = jnp.split(xf, 2, axis=-1)
    return jnp.concatenate([x1 * cos - x2 * sin, x2 * cos + x1 * sin], axis=-1).astype(x.dtype)


def centred_shift(p):
    pad = jnp.pad(p, ((0, 0), (1, 1), (0, 0)))
    return 0.5 * (pad[:, :-2] + pad[:, 2:])


def block_attention(q, k, v):
    B, T, H, dqk = q.shape
    scale = dqk ** -0.5
    nblk = T // Q_BLOCK
    qb = q.reshape(B, nblk, Q_BLOCK, H, dqk).transpose(1, 0, 2, 3, 4)

    def one_block(qblk):
        s = jnp.einsum('bqhd,bkhd->bhqk', qblk, k).astype(jnp.float32) * scale
        pr = jax.nn.softmax(s, axis=-1).astype(v.dtype)
        return jnp.einsum('bhqk,bkhd->bqhd', pr, v)

    out = lax.map(one_block, qb)
    return out.transpose(1, 0, 2, 3, 4).reshape(B, T, H * v.shape[-1])


def mla_branch(p_mla, q_a_norm, kv_a_norm, w_uq, w_ukv):
    B, T, _ = p_mla.shape
    cq = rmsnorm(p_mla[..., :Q_LORA], q_a_norm)
    ckv = rmsnorm(p_mla[..., Q_LORA:Q_LORA + KV_LORA], kv_a_norm)
    k_pe = p_mla[..., Q_LORA + KV_LORA:]
    q = (cq @ w_uq).reshape(B, T, MLA_HEADS, QK_NOPE + QK_ROPE)
    kv = (ckv @ w_ukv).reshape(B, T, MLA_HEADS, QK_NOPE + V_HEAD)
    q_nope, q_pe = q[..., :QK_NOPE], q[..., QK_NOPE:]
    k_nope, v = kv[..., :QK_NOPE], kv[..., QK_NOPE:]
    cos, sin = rope_tables(T)
    q_pe = apply_rope(q_pe, cos[:, None, :], sin[:, None, :])
    k_pe = apply_rope(k_pe, cos, sin)
    q = jnp.concatenate([q_nope, q_pe], axis=-1)
    k = jnp.concatenate([k_nope, jnp.broadcast_to(k_pe[:, :, None, :], (B, T, MLA_HEADS, QK_ROPE))], axis=-1)
    return block_attention(q, k, v)


def wkv_scan(r, w, k, v, kk, a, reverse):
    B, T, H, N = r.shape
    xs = tuple(t.transpose(1, 0, 2, 3) for t in (r, w, k, v, kk, a))

    def step(S, inp):
        r_t, w_t, k_t, v_t, kk_t, a_t = inp
        sa = jnp.einsum('bhvk,bhk->bhv', S, -kk_t)
        S = (S * w_t[:, :, None, :]
             + sa[..., :, None] * (kk_t * a_t)[..., None, :]
             + v_t[..., :, None] * k_t[..., None, :])
        y = jnp.einsum('bhvk,bhk->bhv', S, r_t)
        return S, y

    S0 = jnp.zeros((B, H, N, N), jnp.float32)
    _, ys = lax.scan(step, S0, xs, reverse=reverse)
    return ys.transpose(1, 0, 2, 3)


def rwkv_branch(p_rwkv, mu_shift, w0, w_decay_up, a0, w_iclr_up, w_gate_up, k_k, k_a, r_k, lnx_w, lnx_b):
    B, T, _ = p_rwkv.shape
    pf = p_rwkv.astype(jnp.float32)
    xs = pf + mu_shift.astype(jnp.float32) * (centred_shift(pf) - pf)
    W = RWKV_WIDTH
    r = xs[..., :W]
    k = xs[..., W:2 * W]
    v = xs[..., 2 * W:3 * W]
    wd = xs[..., 3 * W:3 * W + DECAY_LORA]
    ad = xs[..., 3 * W + DECAY_LORA:3 * W + DECAY_LORA + ICLR_LORA]
    gd = xs[..., 3 * W + DECAY_LORA + ICLR_LORA:]
    f32 = lambda t: t.astype(jnp.float32)
    g = jax.nn.sigmoid(gd) @ f32(w_gate_up)
    heads = lambda t: t.reshape(B, T, RWKV_HEADS, RWKV_HEAD)
    kk = heads(k * f32(k_k))
    kk = kk * lax.rsqrt(jnp.sum(kk * kk, axis=-1, keepdims=True) + 1e-12)
    tw = jnp.tanh(wd)
    rh, vh = heads(r), heads(v)
    y = jnp.zeros((B, T, RWKV_HEADS, RWKV_HEAD), jnp.float32)
    bonus_k = jnp.zeros((B, T, RWKV_HEADS, RWKV_HEAD), jnp.float32)
    for d in range(2):
        wlog = -jax.nn.softplus(-(f32(w0[d]) + tw @ f32(w_decay_up[d]))) - 0.5
        decay = jnp.exp(-jnp.exp(wlog))
        a = jax.nn.sigmoid(f32(a0[d]) + ad @ f32(w_iclr_up[d]))
        k_d = k * (1.0 + (a - 1.0) * f32(k_a))
        y = y + wkv_scan(rh, heads(decay), heads(k_d), vh, kk, heads(a), reverse=(d == 1))
        bonus_k = bonus_k + heads(k_d)
    mu = jnp.mean(y, axis=-1, keepdims=True)
    var = jnp.mean(jnp.square(y - mu), axis=-1, keepdims=True)
    yn = ((y - mu) * lax.rsqrt(var + LNX_EPS)).reshape(B, T, W) * f32(lnx_w) + f32(lnx_b)
    bonus = jnp.sum(rh * bonus_k * f32(r_k), axis=-1, keepdims=True) * vh
    out = (yn + bonus.reshape(B, T, W)) * g
    return out.astype(p_rwkv.dtype)


def encoder(x, c, w_ada, b_ada, norm_mix, w_in, q_a_norm, kv_a_norm, w_uq, w_ukv,
            mu_shift, w0, w_decay_up, a0, w_iclr_up, w_gate_up, k_k, k_a, r_k, lnx_w, lnx_b,
            w_mla_o, w_rwkv_o, w_out, norm_ffn, w_ffn_in, w_ffn_out, final_norm):
    for l in range(DEPTH):
        mod = jax.nn.silu(c) @ w_ada[l] + b_ada[l]
        sh1, sc1, g1, sh2, sc2, g2 = jnp.split(mod[:, None, :], 6, axis=-1)
        h = rmsnorm(x, norm_mix[l]) * (1.0 + sc1) + sh1
        p = h @ w_in[l]
        p_mla = p[..., :MLA_IN]
        p_rwkv = p[..., MLA_IN:MLA_IN + RWKV_IN]
        p_gate = p[..., MLA_IN + RWKV_IN:]
        o_mla = mla_branch(p_mla, q_a_norm[l], kv_a_norm[l], w_uq[l], w_ukv[l]) @ w_mla_o[l]
        o_rwkv = rwkv_branch(p_rwkv, mu_shift[l], w0[l], w_decay_up[l], a0[l], w_iclr_up[l],
                             w_gate_up[l], k_k[l], k_a[l], r_k[l], lnx_w[l], lnx_b[l]) @ w_rwkv_o[l]
        gm, gr = jnp.split(jax.nn.sigmoid(p_gate), 2, axis=-1)
        x = x + g1 * ((gm * o_mla + gr * o_rwkv) @ w_out[l])
        h = rmsnorm(x, norm_ffn[l]) * (1.0 + sc2) + sh2
        u, z = jnp.split(h @ w_ffn_in[l], 2, axis=-1)
        x = x + g2 * ((jax.nn.silu(u) * z) @ w_ffn_out[l])
    return rmsnorm(x, final_norm)


def setup_inputs(seed: int = 0) -> dict:
    key = jax.random.key(seed)
    ks = jax.random.split(key, 32)
    L = DEPTH

    def nrm(k, shape, scale):
        return jax.random.normal(k, shape, jnp.float32) * scale

    def gain(k, shape):
        return 1.0 + nrm(k, shape, 0.02)

    w0_base = jnp.linspace(-6.0, -1.0, RWKV_WIDTH, dtype=jnp.float32)
    return {
        "x_prompt": nrm(ks[0], (BATCH, SEQ, D_MODEL), 1.0),
        "x_sample": nrm(ks[1], (DEC_BATCH, DEC_SEQ, D_MODEL), 1.0),
        "c_prompt": nrm(ks[2], (BATCH, D_MODEL), 1.0),
        "c_sample": nrm(ks[3], (DEC_BATCH, D_MODEL), 1.0),
        "w_ada": nrm(ks[4], (L, D_MODEL, 6 * D_MODEL), 0.5 * D_MODEL ** -0.5),
        "b_ada": nrm(ks[5], (L, 6 * D_MODEL), 0.01),
        "norm_mix": gain(ks[6], (L, D_MODEL)),
        "w_in": nrm(ks[7], (L, D_MODEL, IN_COLS), D_MODEL ** -0.5),
        "q_a_norm": gain(ks[8], (L, Q_LORA)),
        "kv_a_norm": gain(ks[9], (L, KV_LORA)),
        "w_uq": nrm(ks[10], (L, Q_LORA, MLA_HEADS * (QK_NOPE + QK_ROPE)), Q_LORA ** -0.5),
        "w_ukv": nrm(ks[11], (L, KV_LORA, MLA_HEADS * (QK_NOPE + V_HEAD)), KV_LORA ** -0.5),
        "mu_shift": jax.random.uniform(ks[12], (L, RWKV_IN), jnp.float32, 0.0, 1.0),
        "w0": w0_base[None, None, :] + nrm(ks[13], (L, 2, RWKV_WIDTH), 0.1),
        "w_decay_up": nrm(ks[14], (L, 2, DECAY_LORA, RWKV_WIDTH), 0.1 * DECAY_LORA ** -0.5),
        "a0": nrm(ks[15], (L, 2, RWKV_WIDTH), 0.1),
        "w_iclr_up": nrm(ks[16], (L, 2, ICLR_LORA, RWKV_WIDTH), ICLR_LORA ** -0.5),
        "w_gate_up": nrm(ks[17], (L, GATE_LORA, RWKV_WIDTH), GATE_LORA ** -0.5),
        "k_k": 0.85 + nrm(ks[18], (L, RWKV_WIDTH), 0.05),
        "k_a": 1.0 + nrm(ks[19], (L, RWKV_WIDTH), 0.05),
        "r_k": nrm(ks[20], (L, RWKV_HEADS, RWKV_HEAD), 0.1),
        "lnx_w": gain(ks[21], (L, RWKV_WIDTH)),
        "lnx_b": nrm(ks[22], (L, RWKV_WIDTH), 0.01),
        "w_mla_o": nrm(ks[23], (L, MLA_WIDTH, D_MODEL), MLA_WIDTH ** -0.5),
        "w_rwkv_o": nrm(ks[24], (L, RWKV_WIDTH, D_MODEL), RWKV_WIDTH ** -0.5),
        "w_out": nrm(ks[25], (L, D_MODEL, D_MODEL), D_MODEL ** -0.5),
        "norm_ffn": gain(ks[26], (L, D_MODEL)),
        "w_ffn_in": nrm(ks[27], (L, D_MODEL, 2 * D_FF), D_MODEL ** -0.5),
        "w_ffn_out": nrm(ks[28], (L, D_FF, D_MODEL), D_FF ** -0.5),
        "final_norm": gain(ks[29], (D_MODEL,)),
    }


def reference(x_prompt, x_sample, c_prompt, c_sample, w_ada, b_ada, norm_mix, w_in, q_a_norm,
              kv_a_norm, w_uq, w_ukv, mu_shift, w0, w_decay_up, a0, w_iclr_up, w_gate_up, k_k,
              k_a, r_k, lnx_w, lnx_b, w_mla_o, w_rwkv_o, w_out, norm_ffn, w_ffn_in, w_ffn_out,
              final_norm):
    weights = (w_ada, b_ada, norm_mix, w_in, q_a_norm, kv_a_norm, w_uq, w_ukv, mu_shift, w0,
               w_decay_up, a0, w_iclr_up, w_gate_up, k_k, k_a, r_k, lnx_w, lnx_b, w_mla_o,
               w_rwkv_o, w_out, norm_ffn, w_ffn_in, w_ffn_out, final_norm)
    y_prompt = encoder(x_prompt, c_prompt, *weights)
    y_sample = encoder(x_sample, c_sample, *weights)
    return (y_prompt, y_sample)
```

```python
import functools

import jax
import jax.numpy as jnp
from jax import lax
from jax.experimental import pallas as pl
from jax.experimental.pallas import tpu as pltpu

F32 = jnp.float32
BF16 = jnp.bfloat16

D_MODEL = 1024
MLA_HEADS = 8
QK_NOPE = 64
QK_ROPE = 32
V_HEAD = 64
Q_LORA = 384
KV_LORA = 256
MLA_WIDTH = MLA_HEADS * V_HEAD
ROPE_BASE = 10000.0
RWKV_HEADS = 8
RWKV_HEAD = 64
RWKV_WIDTH = RWKV_HEADS * RWKV_HEAD
DECAY_LORA = 64
ICLR_LORA = 64
GATE_LORA = 128
D_FF = 2816
EPS = 1e-6
LNX_EPS = 64e-5
MLA_IN = Q_LORA + KV_LORA + QK_ROPE
RWKV_IN = 3 * RWKV_WIDTH + DECAY_LORA + ICLR_LORA + GATE_LORA
GATE_IN = 2 * D_MODEL

LANES = 128
HEAD_PAD = LANES
MLA_IN_PAD = Q_LORA + KV_LORA + LANES
IN_COLS_PAD = MLA_IN_PAD + RWKV_IN + GATE_IN
LORA_OFF = 3 * RWKV_WIDTH
GATE_OFF = LORA_OFF + DECAY_LORA + ICLR_LORA
VMEM_PHYS_BYTES = 64 * 1024 * 1024
VMEM_CAP_BYTES = 60000 * 1024
CHUNK = 64


def _vmem_limit(block_bytes, temp_bytes):
    return int(min(2 * block_bytes + temp_bytes, VMEM_CAP_BYTES))


def _nbytes(shape, dtype):
    n = 1
    for s in shape:
        n *= s
    return n * jnp.dtype(dtype).itemsize


def _bf(x):
    return x if x.dtype == BF16 else x.astype(BF16)


def _mm(a, b):
    return jnp.dot(_bf(a), _bf(b), preferred_element_type=F32)


def _mm_nt(a, b):
    return lax.dot_general(_bf(a), _bf(b), (((1,), (1,)), ((), ())), preferred_element_type=F32)


def _mm_tn(a, b):
    return lax.dot_general(_bf(a), _bf(b), (((0,), (0,)), ((), ())), preferred_element_type=F32)


def _split2(x):
    hi = x.astype(BF16)
    lo = (x - hi.astype(F32)).astype(BF16)
    return hi, lo


def _split3(x):
    hi = x.astype(BF16)
    r = x - hi.astype(F32)
    mid = r.astype(BF16)
    lo = (r - mid.astype(F32)).astype(BF16)
    return hi, mid, lo


def _mm3(a, b):
    ah, al = _split2(a)
    bh, bl = _split2(b)
    return _mm(ah, bh) + (_mm(ah, bl) + _mm(al, bh))


def _mm_exact_lhs(a_bf, b):
    b0, b1, b2 = _split3(b)
    return _mm(a_bf, b0) + (_mm(a_bf, b1) + _mm(a_bf, b2))


def _mm_exact_rhs(a, b_bf):
    a0, a1, a2 = _split3(a)
    return _mm(a0, b_bf) + (_mm(a1, b_bf) + _mm(a2, b_bf))


def _head_ones(n, head):
    ri = lax.broadcasted_iota(jnp.int32, (n, n), 0) // head
    ci = lax.broadcasted_iota(jnp.int32, (n, n), 1) // head
    return jnp.where(ri == ci, 1.0, 0.0).astype(BF16)


def _rmsnorm(x, g):
    return x * lax.rsqrt(jnp.mean(x * x, axis=-1, keepdims=True) + EPS) * g


def _softplus(x):
    return jnp.maximum(x, 0.0) + jnp.log(1.0 + jnp.exp(-jnp.abs(x)))


def _mod_kernel(c_ref, w_ref, b_ref, o_ref):
    c = c_ref[...]
    o_ref[...] = _mm3(c * jax.nn.sigmoid(c), w_ref[...]) + b_ref[...]


def _adaln_mod(c_all, w_ada, b_ada):
    rows, n = c_all.shape[0], w_ada.shape[1]
    tn = 1536
    blocks = _nbytes((rows, D_MODEL), F32) + _nbytes((D_MODEL, tn), F32) + _nbytes((rows + 1, tn), F32)
    return pl.pallas_call(
        _mod_kernel,
        grid=(n // tn,),
        in_specs=[
            pl.BlockSpec((rows, D_MODEL), lambda j: (0, 0)),
            pl.BlockSpec((D_MODEL, tn), lambda j: (0, j)),
            pl.BlockSpec((1, tn), lambda j: (0, j)),
        ],
        out_specs=pl.BlockSpec((rows, tn), lambda j: (0, j)),
        out_shape=jax.ShapeDtypeStruct((rows, n), F32),
        compiler_params=pltpu.CompilerParams(
            dimension_semantics=("arbitrary",),
            vmem_limit_bytes=_vmem_limit(blocks, 3 * _nbytes((D_MODEL, tn), F32))),
        name="adaln_mod",
    )(c_all, w_ada, b_ada.reshape(1, n))


def _inproj_kernel(x_ref, mod_ref, nw_ref, w_ref, pm_ref, pr_ref, pg_ref):
    sh = mod_ref[0, :, 0:D_MODEL]
    sc = mod_ref[0, :, D_MODEL:2 * D_MODEL]
    h = (_rmsnorm(x_ref[0], nw_ref[...]) * (1.0 + sc) + sh).astype(BF16)
    a, b = MLA_IN_PAD, MLA_IN_PAD + RWKV_IN
    pm_ref[0] = _mm(h, w_ref[:, 0:a]).astype(BF16)
    pr_ref[0] = _mm(h, w_ref[:, a:b])
    pg_ref[0] = _mm(h, w_ref[:, b:IN_COLS_PAD]).astype(BF16)


def _in_proj(x, mod3, norm_w, w_in_p, tm):
    bsz, t, _ = x.shape
    blocks = (_nbytes((tm, D_MODEL), F32) + _nbytes((D_MODEL, IN_COLS_PAD), BF16)
              + _nbytes((tm, MLA_IN_PAD), BF16) + _nbytes((tm, RWKV_IN), F32) + _nbytes((tm, GATE_IN), BF16))
    return pl.pallas_call(
        _inproj_kernel,
        grid=(bsz, t // tm),
        in_specs=[
            pl.BlockSpec((1, tm, D_MODEL), lambda b, i: (b, i, 0)),
            pl.BlockSpec((1, 1, 6 * D_MODEL), lambda b, i: (b, 0, 0)),
            pl.BlockSpec((1, D_MODEL), lambda b, i: (0, 0)),
            pl.BlockSpec((D_MODEL, IN_COLS_PAD), lambda b, i: (0, 0)),
        ],
        out_specs=[
            pl.BlockSpec((1, tm, MLA_IN_PAD), lambda b, i: (b, i, 0)),
            pl.BlockSpec((1, tm, RWKV_IN), lambda b, i: (b, i, 0)),
            pl.BlockSpec((1, tm, GATE_IN), lambda b, i: (b, i, 0)),
        ],
        out_shape=[
            jax.ShapeDtypeStruct((bsz, t, MLA_IN_PAD), BF16),
            jax.ShapeDtypeStruct((bsz, t, RWKV_IN), F32),
            jax.ShapeDtypeStruct((bsz, t, GATE_IN), BF16),
        ],
        compiler_params=pltpu.CompilerParams(
            dimension_semantics=("parallel", "parallel"),
            vmem_limit_bytes=_vmem_limit(blocks, 2 * _nbytes((tm, IN_COLS_PAD), F32))),
        name="in_proj",
    )(x, mod3, norm_w, w_in_p)


def _mla_prep_kernel(pm_ref, qn_ref, kvn_ref, wq_ref, wk_ref, wvt_ref, rc_ref, rs1_ref, rs2_ref,
                     q_ref, k_ref, vt_ref, *, tk):
    p = pm_ref[0].astype(F32)
    cq = _rmsnorm(p[:, 0:Q_LORA], qn_ref[...]).astype(BF16)
    ckv = _rmsnorm(p[:, Q_LORA:Q_LORA + KV_LORA], kvn_ref[...]).astype(BF16)
    rc, rs1, rs2 = rc_ref[...], rs1_ref[...], rs2_ref[...]

    def rope(x):
        return x * rc + pltpu.roll(x, QK_ROPE // 2, 1) * rs1 + pltpu.roll(x, LANES - QK_ROPE // 2, 1) * rs2

    k_pe = rope(p[:, Q_LORA + KV_LORA:MLA_IN_PAD])
    q = _mm(cq, wq_ref[...])
    kn = _mm(ckv, wk_ref[...])
    vt = _mm_nt(wvt_ref[...], ckv)
    scale = (QK_NOPE + QK_ROPE) ** -0.5
    for h in range(MLA_HEADS):
        sl = slice(h * HEAD_PAD, (h + 1) * HEAD_PAD)
        q_ref[0, h] = (rope(q[:, sl]) * scale).astype(BF16)
        k_ref[0, h] = (kn[:, sl] + k_pe).astype(BF16)
        for c in range(vt.shape[1] // tk):
            vt_ref[0, h, c] = vt[h * V_HEAD:(h + 1) * V_HEAD, c * tk:(c + 1) * tk].astype(BF16)


def _mla_prep(p_mla, q_norm, kv_norm, wq_p, wk_p, wvt, rope_c, rope_s1, rope_s2, tm, tk):
    bsz, t, _ = p_mla.shape
    assert tm % tk == 0
    hp = MLA_HEADS * HEAD_PAD
    blocks = (_nbytes((tm, MLA_IN_PAD), BF16) + _nbytes((Q_LORA + KV_LORA, hp), BF16)
              + _nbytes((MLA_WIDTH, KV_LORA), BF16) + 3 * _nbytes((tm, LANES), F32)
              + 2 * _nbytes((tm, hp), BF16) + _nbytes((MLA_WIDTH, tm), BF16))
    const = lambda b, i: (0, 0)
    return pl.pallas_call(
        functools.partial(_mla_prep_kernel, tk=tk),
        grid=(bsz, t // tm),
        in_specs=[
            pl.BlockSpec((1, tm, MLA_IN_PAD), lambda b, i: (b, i, 0)),
            pl.BlockSpec((1, Q_LORA), const),
            pl.BlockSpec((1, KV_LORA), const),
            pl.BlockSpec((Q_LORA, hp), const),
            pl.BlockSpec((KV_LORA, hp), const),
            pl.BlockSpec((MLA_WIDTH, KV_LORA), const),
            pl.BlockSpec((tm, LANES), lambda b, i: (i, 0)),
            pl.BlockSpec((tm, LANES), lambda b, i: (i, 0)),
            pl.BlockSpec((tm, LANES), lambda b, i: (i, 0)),
        ],
        out_specs=[
            pl.BlockSpec((1, MLA_HEADS, tm, HEAD_PAD), lambda b, i: (b, 0, i, 0)),
            pl.BlockSpec((1, MLA_HEADS, tm, HEAD_PAD), lambda b, i: (b, 0, i, 0)),
            pl.BlockSpec((1, MLA_HEADS, tm // tk, V_HEAD, tk), lambda b, i: (b, 0, i, 0, 0)),
        ],
        out_shape=[
            jax.ShapeDtypeStruct((bsz, MLA_HEADS, t, HEAD_PAD), BF16),
            jax.ShapeDtypeStruct((bsz, MLA_HEADS, t, HEAD_PAD), BF16),
            jax.ShapeDtypeStruct((bsz, MLA_HEADS, t // tk, V_HEAD, tk), BF16),
        ],
        compiler_params=pltpu.CompilerParams(
            dimension_semantics=("parallel", "parallel"),
            vmem_limit_bytes=_vmem_limit(blocks, 6 * _nbytes((tm, hp), F32))),
        name="mla_prep",
    )(p_mla, q_norm, kv_norm, wq_p, wk_p, wvt, rope_c, rope_s1, rope_s2)


def _attn_kernel(q_ref, k_ref, vt_ref, o_ref):
    q = q_ref[0, 0]
    tq = q.shape[0]
    nk, tk = vt_ref.shape[2], vt_ref.shape[4]

    def body(j, carry):
        m, l, acc = carry
        r0 = pl.multiple_of(j * tk, tk)
        st = _mm_nt(k_ref[0, 0, pl.ds(r0, tk), :], q)
        m_new = jnp.maximum(m, jnp.max(st, axis=0, keepdims=True))
        alpha = jnp.exp(m - m_new)
        p = jnp.exp(st - m_new)
        l = alpha * l + jnp.sum(p, axis=0, keepdims=True)
        acc = alpha * acc + _mm(vt_ref[0, 0, j], p)
        return m_new, l, acc

    init = (jnp.full((1, tq), -jnp.inf, F32), jnp.zeros((1, tq), F32), jnp.zeros((V_HEAD, tq), F32))
    _, l, acc = lax.fori_loop(0, nk, body, init)
    o_ref[0] = (acc / l).astype(BF16)


def _attention(q, k, vt, tq):
    bsz, nh, t, _ = q.shape
    nk, tk = vt.shape[2], vt.shape[4]
    blocks = (_nbytes((tq, HEAD_PAD), BF16) + _nbytes((t, HEAD_PAD), BF16) + _nbytes((V_HEAD, t), BF16)
              + _nbytes((V_HEAD, tq), BF16))
    return pl.pallas_call(
        _attn_kernel,
        grid=(bsz, nh, t // tq),
        in_specs=[
            pl.BlockSpec((1, 1, tq, HEAD_PAD), lambda b, h, i: (b, h, i, 0)),
            pl.BlockSpec((1, 1, t, HEAD_PAD), lambda b, h, i: (b, h, 0, 0)),
            pl.BlockSpec((1, 1, nk, V_HEAD, tk), lambda b, h, i: (b, h, 0, 0, 0)),
        ],
        out_specs=pl.BlockSpec((1, V_HEAD, tq), lambda b, h, i: (b, h, i)),
        out_shape=jax.ShapeDtypeStruct((bsz, nh * V_HEAD, t), BF16),
        compiler_params=pltpu.CompilerParams(
            dimension_semantics=("parallel", "parallel", "arbitrary"),
            vmem_limit_bytes=_vmem_limit(blocks, 6 * _nbytes((tk, tq), F32))),
        name="mla_attn",
    )(q, k, vt)


def _shift_mix(p, prev_row, next_row, mu):
    n = p.shape[0]
    rid = lax.broadcasted_iota(jnp.int32, (n, 1), 0)
    up = jnp.where(rid == 0, prev_row, pltpu.roll(p, 1, 0))
    dn = jnp.where(rid == n - 1, next_row, pltpu.roll(p, n - 1, 0))
    return p + mu * (0.5 * (up + dn) - p)


def _halo_rows(hp_ref, hn_ref, blk, nblk):
    prev_row = jnp.where(blk > 0, hp_ref[0, 7:8, :], 0.0)
    next_row = jnp.where(blk < nblk - 1, hn_ref[0, 0:1, :], 0.0)
    return prev_row, next_row


def _halo_specs(rows, t, blk_of):
    per = rows // 8
    last = t // 8 - 1
    prev = lambda *g: (g[-2], jnp.maximum(blk_of(*g) * per - 1, 0), 0)
    nxt = lambda *g: (g[-2], jnp.minimum((blk_of(*g) + 1) * per, last), 0)
    return pl.BlockSpec((1, 8, RWKV_IN), prev), pl.BlockSpec((1, 8, RWKV_IN), nxt)


def _rwkv_scan_kernel(pr_ref, hp_ref, hn_ref, mu_ref, wl_ref, w0_ref, a0_ref, kk_ref, ka_ref, y_ref,
                      r_sc, v_sc, kn_sc, lw_sc, a_sc, kd_sc, h_sc, *, rows):
    d = pl.program_id(0)
    i = pl.program_id(2)
    nblk = pl.num_programs(2)
    blk = i + d * (nblk - 1 - 2 * i)
    c = CHUNK
    w = RWKV_WIDTH

    @pl.when(i == 0)
    def _():
        h_sc[...] = jnp.zeros_like(h_sc)

    prev_row, next_row = _halo_rows(hp_ref, hn_ref, blk, nblk)
    xs = _shift_mix(pr_ref[0], prev_row, next_row, mu_ref[...])
    k = xs[:, w:2 * w]
    z = xs[:, LORA_OFF:LORA_OFF + LANES]
    lane = lax.broadcasted_iota(jnp.int32, z.shape, 1)
    pre = _mm3(jnp.where(lane < DECAY_LORA, jnp.tanh(z), z), wl_ref[0])
    wlog = -_softplus(-(w0_ref[0] + pre[:, 0:w])) - 0.5
    a = jax.nn.sigmoid(a0_ref[0] + pre[:, w:2 * w])
    kn = k * kk_ref[...]
    ones_h = _head_ones(w, RWKV_HEAD)
    kn = kn * lax.rsqrt(_mm_exact_rhs(kn * kn, ones_h) + 1e-12)
    r_sc[...] = xs[:, 0:w]
    v_sc[...] = xs[:, 2 * w:3 * w]
    kn_sc[...] = kn
    lw_sc[...] = -jnp.exp(wlog)
    a_sc[...] = a
    kd_sc[...] = k * (1.0 + (a - 1.0) * ka_ref[...])

    ri = lax.broadcasted_iota(jnp.int32, (2 * c, 2 * c), 0)
    ci = lax.broadcasted_iota(jnp.int32, (2 * c, 2 * c), 1)
    rt, cs = ri % c, ci % c
    before = (rt - cs) * (1 - 2 * d) > 0
    keep = before | ((ri >= c) & (cs == rt))
    rt_c = lax.broadcasted_iota(jnp.int32, (c, c), 0)
    cs_c = lax.broadcasted_iota(jnp.int32, (c, c), 1)
    tri = jnp.where((rt_c - cs_c) * (1 - 2 * d) >= 0, 1.0, 0.0).astype(BF16)
    eye = ri == ci
    lane_c = lax.broadcasted_iota(jnp.int32, (c, LANES), 1)
    lo_half = lane_c < RWKV_HEAD
    zeros_c = jnp.zeros((c, LANES), F32)
    nsub = rows // c

    def head_slab(x, h):
        s = x[:, (h // 2) * LANES:(h // 2 + 1) * LANES]
        if h % 2:
            s = pltpu.roll(s, RWKV_HEAD, 1)
        return jnp.where(lo_half[0:x.shape[0]], s, 0.0)

    def chunk(s, carry):
        sub = s + d * (nsub - 1 - 2 * s)
        rows0 = pl.multiple_of(sub * c, c)
        sl = pl.ds(rows0, c)
        lw = lw_sc[sl, :]
        cum = _mm_exact_lhs(tri, lw)
        tot = jnp.sum(lw, axis=0, keepdims=True)
        kn_c, a_c, kd_c = kn_sc[sl, :], a_sc[sl, :], kd_sc[sl, :]
        e_neg = jnp.exp(-cum)
        e_end = jnp.exp(tot - cum)
        r_t = r_sc[sl, :] * jnp.exp(cum)
        a_t = -kn_c * jnp.exp(cum - lw)
        b_t = kn_c * a_c * e_neg
        k_t = kd_c * e_neg
        b_e = kn_c * a_c * e_end
        k_e = kd_c * e_end
        g_end = jnp.exp(tot)
        v_c = v_sc[sl, :]
        for j in range(RWKV_HEADS // 2):
            y_pair = None
            for h in (2 * j, 2 * j + 1):
                ah, rh, bh, kh = head_slab(a_t, h), head_slab(r_t, h), head_slab(b_t, h), head_slab(k_t, h)
                vh = head_slab(v_c, h)
                g = _mm_nt(jnp.concatenate([ah, rh], 0), jnp.concatenate([bh, kh], 0))
                g = jnp.where(keep, g, 0.0)
                w_top, w_bot = g[0:c], g[c:2 * c]
                l_ab = jnp.where(lo_half, w_top, 0.0)
                l_ak = jnp.where(lo_half, 0.0, w_top)
                x = jnp.concatenate([ah, _mm(l_ak, jnp.concatenate([zeros_c, vh], 0))], 1)
                pw = l_ab
                zeros_x = jnp.zeros_like(x)
                for step in range(6):
                    x = x + _mm(pw, jnp.concatenate([x, zeros_x], 0))
                    if step < 5:
                        pw = _mm(pw, jnp.concatenate([pw, zeros_c], 0))
                f = jnp.concatenate([x, jnp.concatenate([zeros_c, vh], 1)], 0)
                wf = _mm(w_bot, f)
                r_p = rh + wf[:, 0:LANES]
                y_loc = wf[:, LANES:2 * LANES]
                mn = _mm_tn(jnp.concatenate([head_slab(b_e, h), head_slab(k_e, h)], 0), f)
                gdiag = jnp.where(eye, head_slab(g_end, h), 0.0)
                m_top = (mn[:, 0:LANES] + gdiag)[0:c]
                hs = h_sc[h]
                y_h = _mm(r_p, hs) + y_loc
                h_new = _mm3(m_top, hs) + mn[0:c, LANES:2 * LANES]
                h_sc[h] = jnp.concatenate([h_new, zeros_c], 0)
                y_pair = y_h if y_pair is None else y_pair + pltpu.roll(y_h, RWKV_HEAD, 1)
            y_ref[0, 0, sl, j * LANES:(j + 1) * LANES] = y_pair
        return carry

    lax.fori_loop(0, nsub, chunk, 0)


def _rwkv_scan(p_rwkv, mu, w_lora, w0, a0, k_k, k_a, rows):
    bsz, t, _ = p_rwkv.shape
    nblk = t // rows
    w = RWKV_WIDTH
    blk_of = lambda d, b, i: i + d * (nblk - 1 - 2 * i)
    halo_prev, halo_next = _halo_specs(rows, t, blk_of)
    vec = lambda n: pl.BlockSpec((1, n), lambda d, b, i: (0, 0))
    dvec = lambda n: pl.BlockSpec((1, 1, n), lambda d, b, i: (d, 0, 0))
    blocks = (_nbytes((rows + 16, RWKV_IN), F32) + _nbytes((LANES, 2 * w), F32) + _nbytes((rows, w), F32))
    scratch = 6 * _nbytes((rows, w), F32) + _nbytes((RWKV_HEADS, LANES, LANES), F32)
    return pl.pallas_call(
        functools.partial(_rwkv_scan_kernel, rows=rows),
        grid=(2, bsz, nblk),
        in_specs=[
            pl.BlockSpec((1, rows, RWKV_IN), lambda d, b, i: (b, blk_of(d, b, i), 0)),
            halo_prev, halo_next,
            vec(RWKV_IN),
            pl.BlockSpec((1, LANES, 2 * w), lambda d, b, i: (d, 0, 0)),
            dvec(w), dvec(w), vec(w), vec(w),
        ],
        out_specs=pl.BlockSpec((1, 1, rows, w), lambda d, b, i: (d, b, blk_of(d, b, i), 0)),
        out_shape=jax.ShapeDtypeStruct((2, bsz, t, w), F32),
        scratch_shapes=[pltpu.VMEM((rows, w), F32)] * 6 + [pltpu.VMEM((RWKV_HEADS, LANES, LANES), F32)],
        compiler_params=pltpu.CompilerParams(
            dimension_semantics=("parallel", "parallel", "arbitrary"),
            vmem_limit_bytes=_vmem_limit(blocks, scratch + 8 * _nbytes((rows, RWKV_IN), F32))),
        name="rwkv_scan",
    )(p_rwkv, p_rwkv, p_rwkv, mu, w_lora, w0, a0, k_k, k_a)


def _mix_kernel(x_ref, mod_ref, ot_ref, y_ref, pr_ref, hp_ref, hn_ref, pg_ref, mu_ref, wi_ref, a0_ref, ka_ref,
                rk_ref, wg_ref, lw_ref, lb_ref, wmo_ref, wro_ref, wo_ref, o_ref):
    i = pl.program_id(1)
    w = RWKV_WIDTH
    prev_row, next_row = _halo_rows(hp_ref, hn_ref, i, pl.num_programs(1))
    xs = _shift_mix(pr_ref[0], prev_row, next_row, mu_ref[...])
    r, k, v = xs[:, 0:w], xs[:, w:2 * w], xs[:, 2 * w:3 * w]
    pre = _mm3(xs[:, LORA_OFF:LORA_OFF + LANES], wi_ref[...])
    ka = ka_ref[...]
    bonus_k = (k * (1.0 + (jax.nn.sigmoid(a0_ref[0] + pre[:, 0:w]) - 1.0) * ka)
               + k * (1.0 + (jax.nn.sigmoid(a0_ref[1] + pre[:, w:2 * w]) - 1.0) * ka))
    gate = _mm(jax.nn.sigmoid(xs[:, GATE_OFF:GATE_OFF + GATE_LORA]), wg_ref[...])
    ones_h = _head_ones(w, RWKV_HEAD)
    inv_n = 1.0 / RWKV_HEAD
    y = y_ref[0, 0] + y_ref[1, 0]
    mean = _mm_exact_rhs(y, ones_h) * inv_n
    yc = y - mean
    var = _mm_exact_rhs(yc * yc, ones_h) * inv_n
    yn = yc * lax.rsqrt(var + LNX_EPS) * lw_ref[...] + lb_ref[...]
    bonus = _mm_exact_rhs(r * bonus_k * rk_ref[...], ones_h) * v
    o_rwkv = _mm((yn + bonus) * gate, wro_ref[...])
    o_mla = _mm_tn(ot_ref[0], wmo_ref[...])
    pg = pg_ref[0].astype(F32)
    merged = jax.nn.sigmoid(pg[:, 0:D_MODEL]) * o_mla + jax.nn.sigmoid(pg[:, D_MODEL:2 * D_MODEL]) * o_rwkv
    g1 = mod_ref[0, :, 2 * D_MODEL:3 * D_MODEL]
    o_ref[0] = x_ref[0] + g1 * _mm(merged, wo_ref[...])


def _mix_out(x, mod3, o_t, y, p_rwkv, p_gate, mu, w_iclr, a0, k_a, r_k, w_gate, lnx_w, lnx_b,
             w_mla_o, w_rwkv_o, w_out, tm):
    bsz, t, _ = x.shape
    w = RWKV_WIDTH
    halo_prev, halo_next = _halo_specs(tm, t, lambda b, i: i)
    const2 = lambda b, i: (0, 0)
    vec = lambda n: pl.BlockSpec((1, n), const2)
    blocks = (2 * _nbytes((tm, D_MODEL), F32) + _nbytes((MLA_WIDTH, tm), BF16) + 2 * _nbytes((tm, w), F32)
              + _nbytes((tm + 16, RWKV_IN), F32) + _nbytes((tm, GATE_IN), BF16)
              + _nbytes((LANES, 2 * w), F32) + _nbytes((GATE_LORA, w), BF16)
              + 2 * _nbytes((w, D_MODEL), BF16) + _nbytes((D_MODEL, D_MODEL), BF16))
    return pl.pallas_call(
        _mix_kernel,
        grid=(bsz, t // tm),
        in_specs=[
            pl.BlockSpec((1, tm, D_MODEL), lambda b, i: (b, i, 0)),
            pl.BlockSpec((1, 1, 6 * D_MODEL), lambda b, i: (b, 0, 0)),
            pl.BlockSpec((1, MLA_WIDTH, tm), lambda b, i: (b, 0, i)),
            pl.BlockSpec((2, 1, tm, w), lambda b, i: (0, b, i, 0)),
            pl.BlockSpec((1, tm, RWKV_IN), lambda b, i: (b, i, 0)),
            halo_prev, halo_next,
            pl.BlockSpec((1, tm, GATE_IN), lambda b, i: (b, i, 0)),
            vec(RWKV_IN),
            pl.BlockSpec((LANES, 2 * w), const2),
            pl.BlockSpec((2, 1, w), lambda b, i: (0, 0, 0)),
            vec(w), vec(w),
            pl.BlockSpec((GATE_LORA, w), const2),
            vec(w), vec(w),
            pl.BlockSpec((MLA_WIDTH, D_MODEL), const2),
            pl.BlockSpec((w, D_MODEL), const2),
            pl.BlockSpec((D_MODEL, D_MODEL), const2),
        ],
        out_specs=pl.BlockSpec((1, tm, D_MODEL), lambda b, i: (b, i, 0)),
        out_shape=jax.ShapeDtypeStruct((bsz, t, D_MODEL), F32),
        compiler_params=pltpu.CompilerParams(
            dimension_semantics=("parallel", "parallel"),
            vmem_limit_bytes=_vmem_limit(blocks, 10 * _nbytes((tm, RWKV_IN), F32))),
        name="mix_out",
    )(x, mod3, o_t, y, p_rwkv, p_rwkv, p_rwkv, p_gate, mu, w_iclr, a0, k_a, r_k, w_gate, lnx_w, lnx_b,
      w_mla_o, w_rwkv_o, w_out)


def _ffn_kernel(x_ref, mod_ref, nw_ref, wi_ref, wo_ref, fn_ref, o_ref, acc_ref, *, tf):
    x = x_ref[0]
    sh = mod_ref[0, :, 3 * D_MODEL:4 * D_MODEL]
    sc = mod_ref[0, :, 4 * D_MODEL:5 * D_MODEL]
    g2 = mod_ref[0, :, 5 * D_MODEL:6 * D_MODEL]
    h = (_rmsnorm(x, nw_ref[...]) * (1.0 + sc) + sh).astype(BF16)
    for j in range(D_FF // tf):
        u = _mm(h, wi_ref[:, j * tf:(j + 1) * tf])
        zg = _mm(h, wi_ref[:, D_FF + j * tf:D_FF + (j + 1) * tf])
        part = _mm(u * jax.nn.sigmoid(u) * zg, wo_ref[j * tf:(j + 1) * tf, :])
        if j == 0:
            acc_ref[...] = part
        else:
            acc_ref[...] += part
    o_ref[0] = _rmsnorm(x + g2 * acc_ref[...], fn_ref[...])


def _ffn(x, mod3, norm_w, w_in, w_out, final_w, tm, tf):
    bsz, t, _ = x.shape
    blocks = (2 * _nbytes((tm, D_MODEL), F32) + _nbytes((D_MODEL, 2 * D_FF), BF16)
              + _nbytes((D_FF, D_MODEL), BF16))
    const2 = lambda b, i: (0, 0)
    return pl.pallas_call(
        functools.partial(_ffn_kernel, tf=tf),
        grid=(bsz, t // tm),
        in_specs=[
            pl.BlockSpec((1, tm, D_MODEL), lambda b, i: (b, i, 0)),
            pl.BlockSpec((1, 1, 6 * D_MODEL), lambda b, i: (b, 0, 0)),
            pl.BlockSpec((1, D_MODEL), const2),
            pl.BlockSpec((D_MODEL, 2 * D_FF), const2),
            pl.BlockSpec((D_FF, D_MODEL), const2),
            pl.BlockSpec((1, D_MODEL), const2),
        ],
        out_specs=pl.BlockSpec((1, tm, D_MODEL), lambda b, i: (b, i, 0)),
        out_shape=jax.ShapeDtypeStruct((bsz, t, D_MODEL), F32),
        scratch_shapes=[pltpu.VMEM((tm, D_MODEL), F32)],
        compiler_params=pltpu.CompilerParams(
            dimension_semantics=("parallel", "parallel"),
            vmem_limit_bytes=_vmem_limit(blocks, _nbytes((tm, D_MODEL), F32) + 6 * _nbytes((tm, tf), F32))),
        name="ffn",
    )(x, mod3, norm_w, w_in, w_out, final_w)


def _rope_tables(t):
    half = QK_ROPE // 2
    inv = ROPE_BASE ** (-jnp.arange(half, dtype=F32) / half)
    ang = jnp.arange(t, dtype=F32)[:, None] * inv[None, :]
    cos, sin = jnp.cos(ang), jnp.sin(ang)
    z = lambda n: jnp.zeros((t, n), F32)
    rc = jnp.concatenate([jnp.ones((t, QK_NOPE), F32), cos, cos, z(LANES - QK_NOPE - QK_ROPE)], 1)
    rs1 = jnp.concatenate([z(QK_NOPE + half), sin, z(LANES - QK_NOPE - QK_ROPE)], 1)
    rs2 = jnp.concatenate([z(QK_NOPE), -sin, z(LANES - QK_NOPE - half)], 1)
    return rc, rs1, rs2


def _prepare_params(w_in, w_uq, w_ukv, w_decay_up, w_iclr_up, w_gate_up, w_mla_o, w_rwkv_o, w_out,
                    w_ffn_in, w_ffn_out):
    dqk = QK_NOPE + QK_ROPE
    kpe_tile = jnp.zeros((D_MODEL, LANES), F32).at[:, QK_NOPE:QK_NOPE + QK_ROPE].set(
        w_in[:, Q_LORA + KV_LORA:MLA_IN])
    w_in_p = jnp.concatenate([w_in[:, 0:Q_LORA + KV_LORA], kpe_tile, w_in[:, MLA_IN:]], 1).astype(BF16)
    wq = w_uq.reshape(Q_LORA, MLA_HEADS, dqk)
    wq_p = jnp.pad(wq, ((0, 0), (0, 0), (0, HEAD_PAD - dqk))).reshape(Q_LORA, MLA_HEADS * HEAD_PAD).astype(BF16)
    wkv = w_ukv.reshape(KV_LORA, MLA_HEADS, QK_NOPE + V_HEAD)
    wk_p = jnp.pad(wkv[:, :, 0:QK_NOPE], ((0, 0), (0, 0), (0, HEAD_PAD - QK_NOPE))).reshape(
        KV_LORA, MLA_HEADS * HEAD_PAD).astype(BF16)
    wvt = wkv[:, :, QK_NOPE:].reshape(KV_LORA, MLA_WIDTH).T.astype(BF16)
    zl = jnp.zeros((DECAY_LORA, RWKV_WIDTH), F32)
    w_lora = jnp.stack([jnp.concatenate([jnp.concatenate([w_decay_up[d], zl], 1),
                                         jnp.concatenate([zl, w_iclr_up[d]], 1)], 0) for d in range(2)])
    w_iclr = jnp.concatenate([jnp.zeros((DECAY_LORA, 2 * RWKV_WIDTH), F32),
                              jnp.concatenate([w_iclr_up[0], w_iclr_up[1]], 1)], 0)
    return dict(w_in_p=w_in_p, wq_p=wq_p, wk_p=wk_p, wvt=wvt, w_lora=w_lora, w_iclr=w_iclr,
                w_gate=w_gate_up.astype(BF16), w_mla_o=w_mla_o.astype(BF16), w_rwkv_o=w_rwkv_o.astype(BF16),
                w_out=w_out.astype(BF16), w_ffn_in=w_ffn_in.astype(BF16), w_ffn_out=w_ffn_out.astype(BF16))


def _tiles(t):
    return dict(tm_proj=min(256, t), tm_prep=min(512, t), tq=min(256, t), tk=min(512, t),
                rows_scan=min(256, t), tm_mix=min(256, t), tm_ffn=min(256, t), tf=256)


def _encoder(x, mod, pp, norm_mix, q_a_norm, kv_a_norm, mu_shift, w0, a0, k_k, k_a, r_k, lnx_w, lnx_b,
             norm_ffn, final_norm):
    bsz, t, _ = x.shape
    ts = _tiles(t)
    row = lambda v: v.reshape(1, -1)
    mod3 = mod.reshape(bsz, 1, 6 * D_MODEL)
    p_mla, p_rwkv, p_gate = _in_proj(x, mod3, row(norm_mix), pp["w_in_p"], ts["tm_proj"])
    q, k, vt = _mla_prep(p_mla, row(q_a_norm), row(kv_a_norm), pp["wq_p"], pp["wk_p"], pp["wvt"],
                         *_rope_tables(t), ts["tm_prep"], ts["tk"])
    o_t = _attention(q, k, vt, ts["tq"])
    y = _rwkv_scan(p_rwkv, row(mu_shift), pp["w_lora"], w0.reshape(2, 1, -1), a0.reshape(2, 1, -1),
                   row(k_k), row(k_a), ts["rows_scan"])
    x1 = _mix_out(x, mod3, o_t, y, p_rwkv, p_gate, row(mu_shift), pp["w_iclr"], a0.reshape(2, 1, -1), row(k_a),
                  row(r_k), pp["w_gate"], row(lnx_w), row(lnx_b), pp["w_mla_o"], pp["w_rwkv_o"], pp["w_out"],
                  ts["tm_mix"])
    return _ffn(x1, mod3, row(norm_ffn), pp["w_ffn_in"], pp["w_ffn_out"], row(final_norm), ts["tm_ffn"], ts["tf"])


def kernel(x_prompt, x_sample, c_prompt, c_sample, w_ada, b_ada, norm_mix, w_in, q_a_norm, kv_a_norm, w_uq, w_ukv, mu_shift, w0, w_decay_up, a0, w_iclr_up, w_gate_up, k_k, k_a, r_k, lnx_w, lnx_b, w_mla_o, w_rwkv_o, w_out, norm_ffn, w_ffn_in, w_ffn_out, final_norm):
    pp = _prepare_params(w_in[0], w_uq[0], w_ukv[0], w_decay_up[0], w_iclr_up[0], w_gate_up[0], w_mla_o[0],
                         w_rwkv_o[0], w_out[0], w_ffn_in[0], w_ffn_out[0])
    nb = x_prompt.shape[0]
    mod = _adaln_mod(jnp.concatenate([c_prompt, c_sample], 0), w_ada[0], b_ada[0])
    args = (pp, norm_mix[0], q_a_norm[0], kv_a_norm[0], mu_shift[0], w0[0], a0[0], k_k[0], k_a[0], r_k[0],
            lnx_w[0], lnx_b[0], norm_ffn[0], final_norm)
    return (_encoder(x_prompt, mod[:nb], *args), _encoder(x_sample, mod[nb:], *args))
```

```python
import functools

import jax
import jax.numpy as jnp
from jax import lax
from jax.experimental import pallas as pl
from jax.experimental.pallas import tpu as pltpu

F32 = jnp.float32
BF16 = jnp.bfloat16

D_MODEL = 1024
MLA_HEADS = 8
QK_NOPE = 64
QK_ROPE = 32
V_HEAD = 64
V_ROWS = 80
Q_LORA = 384
KV_LORA = 256
MLA_WIDTH = MLA_HEADS * V_HEAD
ROPE_BASE = 10000.0
RWKV_HEADS = 8
RWKV_HEAD = 64
RWKV_WIDTH = RWKV_HEADS * RWKV_HEAD
DECAY_LORA = 64
ICLR_LORA = 64
GATE_LORA = 128
D_FF = 2816
EPS = 1e-6
LNX_EPS = 64e-5
MLA_IN = Q_LORA + KV_LORA + QK_ROPE
RWKV_IN = 3 * RWKV_WIDTH + DECAY_LORA + ICLR_LORA + GATE_LORA
GATE_IN = 2 * D_MODEL

LANES = 128
HEAD_PAD = LANES
MLA_IN_PAD = Q_LORA + KV_LORA + LANES
IN_COLS_PAD = MLA_IN_PAD + RWKV_IN + GATE_IN
LORA_OFF = 3 * RWKV_WIDTH
GATE_OFF = LORA_OFF + DECAY_LORA + ICLR_LORA
VMEM_PHYS_BYTES = 64 * 1024 * 1024
VMEM_CAP_BYTES = 60000 * 1024
LOG2E = 1.4426950408889634
CHUNK = 64


def _vmem_limit(block_bytes, temp_bytes):
    return int(min(2 * block_bytes + temp_bytes, VMEM_CAP_BYTES))


def _nbytes(shape, dtype):
    n = 1
    for s in shape:
        n *= s
    return n * jnp.dtype(dtype).itemsize


def _bf(x):
    return x if x.dtype == BF16 else x.astype(BF16)


def _mm(a, b):
    return jnp.dot(_bf(a), _bf(b), preferred_element_type=F32)


def _mm_nt(a, b):
    return lax.dot_general(_bf(a), _bf(b), (((1,), (1,)), ((), ())), preferred_element_type=F32)


def _mm_tn(a, b):
    return lax.dot_general(_bf(a), _bf(b), (((0,), (0,)), ((), ())), preferred_element_type=F32)


def _split2(x):
    hi = x.astype(BF16)
    lo = (x - hi.astype(F32)).astype(BF16)
    return hi, lo


def _split3(x):
    hi = x.astype(BF16)
    r = x - hi.astype(F32)
    mid = r.astype(BF16)
    lo = (r - mid.astype(F32)).astype(BF16)
    return hi, mid, lo


def _mm3(a, b):
    ah, al = _split2(a)
    bh, bl = _split2(b)
    return _mm(ah, bh) + (_mm(ah, bl) + _mm(al, bh))


def _mm_exact_lhs(a_bf, b):
    b0, b1, b2 = _split3(b)
    return _mm(a_bf, b0) + (_mm(a_bf, b1) + _mm(a_bf, b2))


def _mm_exact_rhs(a, b_bf):
    a0, a1, a2 = _split3(a)
    return _mm(a0, b_bf) + (_mm(a1, b_bf) + _mm(a2, b_bf))


def _head_ones(n, head):
    ri = lax.broadcasted_iota(jnp.int32, (n, n), 0) // head
    ci = lax.broadcasted_iota(jnp.int32, (n, n), 1) // head
    return jnp.where(ri == ci, 1.0, 0.0).astype(BF16)


def _rmsnorm(x, g):
    return x * lax.rsqrt(jnp.mean(x * x, axis=-1, keepdims=True) + EPS) * g


def _softplus(x):
    return jnp.maximum(x, 0.0) + jnp.log(1.0 + jnp.exp(-jnp.abs(x)))


def _mod_kernel(c_ref, w_ref, b_ref, o_ref):
    c = c_ref[...]
    o_ref[...] = _mm3(c * jax.nn.sigmoid(c), w_ref[...]) + b_ref[...]


def _adaln_mod(c_all, w_ada, b_ada):
    rows, n = c_all.shape[0], w_ada.shape[1]
    tn = 1536
    blocks = _nbytes((rows, D_MODEL), F32) + _nbytes((D_MODEL, tn), F32) + _nbytes((rows + 1, tn), F32)
    return pl.pallas_call(
        _mod_kernel,
        grid=(n // tn,),
        in_specs=[
            pl.BlockSpec((rows, D_MODEL), lambda j: (0, 0)),
            pl.BlockSpec((D_MODEL, tn), lambda j: (0, j)),
            pl.BlockSpec((1, tn), lambda j: (0, j)),
        ],
        out_specs=pl.BlockSpec((rows, tn), lambda j: (0, j)),
        out_shape=jax.ShapeDtypeStruct((rows, n), F32),
        compiler_params=pltpu.CompilerParams(
            dimension_semantics=("arbitrary",),
            vmem_limit_bytes=_vmem_limit(blocks, 3 * _nbytes((D_MODEL, tn), F32))),
        name="adaln_mod",
    )(c_all, w_ada, b_ada.reshape(1, n))


def _inproj_kernel(x_ref, mod_ref, nw_ref, w_ref, pm_ref, pr_ref, pg_ref):
    sh = mod_ref[0, :, 0:D_MODEL]
    sc = mod_ref[0, :, D_MODEL:2 * D_MODEL]
    h = (_rmsnorm(x_ref[0], nw_ref[...]) * (1.0 + sc) + sh).astype(BF16)
    a, b = MLA_IN_PAD, MLA_IN_PAD + RWKV_IN
    pm_ref[0] = _mm(h, w_ref[:, 0:a]).astype(BF16)
    pr_ref[0] = _mm(h, w_ref[:, a:b])
    pg_ref[0] = _mm(h, w_ref[:, b:IN_COLS_PAD]).astype(BF16)


def _in_proj(x, mod3, norm_w, w_in_p, tm):
    bsz, t, _ = x.shape
    blocks = (_nbytes((tm, D_MODEL), F32) + _nbytes((D_MODEL, IN_COLS_PAD), BF16)
              + _nbytes((tm, MLA_IN_PAD), BF16) + _nbytes((tm, RWKV_IN), F32) + _nbytes((tm, GATE_IN), BF16))
    return pl.pallas_call(
        _inproj_kernel,
        grid=(bsz, t // tm),
        in_specs=[
            pl.BlockSpec((1, tm, D_MODEL), lambda b, i: (b, i, 0)),
            pl.BlockSpec((1, 1, 6 * D_MODEL), lambda b, i: (b, 0, 0)),
            pl.BlockSpec((1, D_MODEL), lambda b, i: (0, 0)),
            pl.BlockSpec((D_MODEL, IN_COLS_PAD), lambda b, i: (0, 0)),
        ],
        out_specs=[
            pl.BlockSpec((1, tm, MLA_IN_PAD), lambda b, i: (b, i, 0)),
            pl.BlockSpec((1, tm, RWKV_IN), lambda b, i: (b, i, 0)),
            pl.BlockSpec((1, tm, GATE_IN), lambda b, i: (b, i, 0)),
        ],
        out_shape=[
            jax.ShapeDtypeStruct((bsz, t, MLA_IN_PAD), BF16),
            jax.ShapeDtypeStruct((bsz, t, RWKV_IN), F32),
            jax.ShapeDtypeStruct((bsz, t, GATE_IN), BF16),
        ],
        compiler_params=pltpu.CompilerParams(
            dimension_semantics=("parallel", "parallel"),
            vmem_limit_bytes=_vmem_limit(blocks, 2 * _nbytes((tm, IN_COLS_PAD), F32))),
        name="in_proj",
    )(x, mod3, norm_w, w_in_p)


def _mla_prep_kernel(pm_ref, qn_ref, kvn_ref, wq_ref, wk_ref, wvt_ref, rc_ref, rs1_ref, rs2_ref,
                     q_ref, k_ref, vt_ref, *, tk):
    p = pm_ref[0].astype(F32)
    cq = _rmsnorm(p[:, 0:Q_LORA], qn_ref[...]).astype(BF16)
    ckv = _rmsnorm(p[:, Q_LORA:Q_LORA + KV_LORA], kvn_ref[...]).astype(BF16)
    rc, rs1, rs2 = rc_ref[...], rs1_ref[...], rs2_ref[...]

    def rope(x):
        return x * rc + pltpu.roll(x, QK_ROPE // 2, 1) * rs1 + pltpu.roll(x, LANES - QK_ROPE // 2, 1) * rs2

    k_pe = rope(p[:, Q_LORA + KV_LORA:MLA_IN_PAD])
    q = _mm(cq, wq_ref[...])
    kn = _mm(ckv, wk_ref[...])
    vt = _mm_nt(wvt_ref[...], ckv)
    scale = (QK_NOPE + QK_ROPE) ** -0.5 * LOG2E
    tail = jnp.where(lax.broadcasted_iota(jnp.int32, (V_ROWS - V_HEAD, tk), 0) == 0, 1.0, 0.0)
    for h in range(MLA_HEADS):
        sl = slice(h * HEAD_PAD, (h + 1) * HEAD_PAD)
        q_ref[0, h] = (rope(q[:, sl]) * scale).astype(BF16)
        k_ref[0, h] = (kn[:, sl] + k_pe).astype(BF16)
        for c in range(vt.shape[1] // tk):
            vt_ref[0, h, c] = jnp.concatenate(
                [vt[h * V_HEAD:(h + 1) * V_HEAD, c * tk:(c + 1) * tk], tail], 0).astype(BF16)


def _mla_prep(p_mla, q_norm, kv_norm, wq_p, wk_p, wvt, rope_c, rope_s1, rope_s2, tm, tk):
    bsz, t, _ = p_mla.shape
    assert tm % tk == 0
    hp = MLA_HEADS * HEAD_PAD
    blocks = (_nbytes((tm, MLA_IN_PAD), BF16) + _nbytes((Q_LORA + KV_LORA, hp), BF16)
              + _nbytes((MLA_WIDTH, KV_LORA), BF16) + 3 * _nbytes((tm, LANES), F32)
              + 2 * _nbytes((tm, hp), BF16) + _nbytes((MLA_WIDTH, tm), BF16))
    const = lambda b, i: (0, 0)
    return pl.pallas_call(
        functools.partial(_mla_prep_kernel, tk=tk),
        grid=(bsz, t // tm),
        in_specs=[
            pl.BlockSpec((1, tm, MLA_IN_PAD), lambda b, i: (b, i, 0)),
            pl.BlockSpec((1, Q_LORA), const),
            pl.BlockSpec((1, KV_LORA), const),
            pl.BlockSpec((Q_LORA, hp), const),
            pl.BlockSpec((KV_LORA, hp), const),
            pl.BlockSpec((MLA_WIDTH, KV_LORA), const),
            pl.BlockSpec((tm, LANES), lambda b, i: (i, 0)),
            pl.BlockSpec((tm, LANES), lambda b, i: (i, 0)),
            pl.BlockSpec((tm, LANES), lambda b, i: (i, 0)),
        ],
        out_specs=[
            pl.BlockSpec((1, MLA_HEADS, tm, HEAD_PAD), lambda b, i: (b, 0, i, 0)),
            pl.BlockSpec((1, MLA_HEADS, tm, HEAD_PAD), lambda b, i: (b, 0, i, 0)),
            pl.BlockSpec((1, MLA_HEADS, tm // tk, V_ROWS, tk), lambda b, i: (b, 0, i, 0, 0)),
        ],
        out_shape=[
            jax.ShapeDtypeStruct((bsz, MLA_HEADS, t, HEAD_PAD), BF16),
            jax.ShapeDtypeStruct((bsz, MLA_HEADS, t, HEAD_PAD), BF16),
            jax.ShapeDtypeStruct((bsz, MLA_HEADS, t // tk, V_ROWS, tk), BF16),
        ],
        compiler_params=pltpu.CompilerParams(
            dimension_semantics=("parallel", "parallel"),
            vmem_limit_bytes=_vmem_limit(blocks, 6 * _nbytes((tm, hp), F32))),
        name="mla_prep",
    )(p_mla, q_norm, kv_norm, wq_p, wk_p, wvt, rope_c, rope_s1, rope_s2)


def _attn_kernel(q_ref, k_ref, vt_ref, o_ref):
    q = q_ref[0, 0]
    tq = q.shape[0]
    nk, tk = vt_ref.shape[2], vt_ref.shape[4]

    def body(j, carry):
        m, acc = carry
        r0 = pl.multiple_of(j * tk, tk)
        st = _mm_nt(k_ref[0, 0, pl.ds(r0, tk), :], q)
        m_new = jnp.maximum(m, jnp.max(st, axis=0, keepdims=True))
        p = jnp.exp2(st - m_new)
        acc = jnp.exp2(m - m_new) * acc + _mm(vt_ref[0, 0, j], p)
        return m_new, acc

    init = (jnp.full((1, tq), -jnp.inf, F32), jnp.zeros((V_ROWS, tq), F32))
    _, acc = lax.fori_loop(0, nk, body, init)
    o_ref[0] = (acc[0:V_HEAD] / acc[V_HEAD:V_HEAD + 1]).astype(BF16)


def _attention(q, k, vt, tq):
    bsz, nh, t, _ = q.shape
    nk, tk = vt.shape[2], vt.shape[4]
    blocks = (_nbytes((tq, HEAD_PAD), BF16) + _nbytes((t, HEAD_PAD), BF16) + _nbytes((V_ROWS, t), BF16)
              + _nbytes((V_HEAD, tq), BF16))
    return pl.pallas_call(
        _attn_kernel,
        grid=(bsz, nh, t // tq),
        in_specs=[
            pl.BlockSpec((1, 1, tq, HEAD_PAD), lambda b, h, i: (b, h, i, 0)),
            pl.BlockSpec((1, 1, t, HEAD_PAD), lambda b, h, i: (b, h, 0, 0)),
            pl.BlockSpec((1, 1, nk, V_ROWS, tk), lambda b, h, i: (b, h, 0, 0, 0)),
        ],
        out_specs=pl.BlockSpec((1, V_HEAD, tq), lambda b, h, i: (b, h, i)),
        out_shape=jax.ShapeDtypeStruct((bsz, nh * V_HEAD, t), BF16),
        compiler_params=pltpu.CompilerParams(
            dimension_semantics=("parallel", "parallel", "arbitrary"),
            vmem_limit_bytes=_vmem_limit(blocks, 6 * _nbytes((tk, tq), F32))),
        name="mla_attn",
    )(q, k, vt)


def _shift_mix(p, prev_row, next_row, mu):
    n = p.shape[0]
    rid = lax.broadcasted_iota(jnp.int32, (n, 1), 0)
    up = jnp.where(rid == 0, prev_row, pltpu.roll(p, 1, 0))
    dn = jnp.where(rid == n - 1, next_row, pltpu.roll(p, n - 1, 0))
    return p + mu * (0.5 * (up + dn) - p)


def _halo_rows(hp_ref, hn_ref, blk, nblk):
    prev_row = jnp.where(blk > 0, hp_ref[0, 7:8, :], 0.0)
    next_row = jnp.where(blk < nblk - 1, hn_ref[0, 0:1, :], 0.0)
    return prev_row, next_row


def _halo_specs(rows, t, blk_of):
    per = rows // 8
    last = t // 8 - 1
    prev = lambda *g: (g[-2], jnp.maximum(blk_of(*g) * per - 1, 0), 0)
    nxt = lambda *g: (g[-2], jnp.minimum((blk_of(*g) + 1) * per, last), 0)
    return pl.BlockSpec((1, 8, RWKV_IN), prev), pl.BlockSpec((1, 8, RWKV_IN), nxt)


def _rwkv_scan_kernel(pr_ref, hp_ref, hn_ref, mu_ref, wl_ref, w0_ref, a0_ref, kk_ref, ka_ref, y_ref,
                      r_sc, v_sc, kn_sc, lw_sc, a_sc, kd_sc, h_sc, rp_sc, yl_sc, mt_sc, nt_sc, *, rows):
    d = pl.program_id(0)
    i = pl.program_id(2)
    nblk = pl.num_programs(2)
    blk = i + d * (nblk - 1 - 2 * i)
    c = CHUNK
    w = RWKV_WIDTH

    @pl.when(i == 0)
    def _():
        h_sc[...] = jnp.zeros_like(h_sc)

    prev_row, next_row = _halo_rows(hp_ref, hn_ref, blk, nblk)
    xs = _shift_mix(pr_ref[0], prev_row, next_row, mu_ref[...])
    k = xs[:, w:2 * w]
    z = xs[:, LORA_OFF:LORA_OFF + LANES]
    lane = lax.broadcasted_iota(jnp.int32, z.shape, 1)
    pre = _mm3(jnp.where(lane < DECAY_LORA, jnp.tanh(z), z), wl_ref[0])
    wlog = -_softplus(-(w0_ref[0] + pre[:, 0:w])) - 0.5
    a = jax.nn.sigmoid(a0_ref[0] + pre[:, w:2 * w])
    kn = k * kk_ref[...]
    ones_h = _head_ones(w, RWKV_HEAD)
    kn = kn * lax.rsqrt(_mm_exact_rhs(kn * kn, ones_h) + 1e-12)
    r_sc[...] = xs[:, 0:w]
    v_sc[...] = xs[:, 2 * w:3 * w]
    kn_sc[...] = kn
    lw_sc[...] = -jnp.exp(wlog)
    a_sc[...] = a
    kd_sc[...] = k * (1.0 + (a - 1.0) * ka_ref[...])

    ri = lax.broadcasted_iota(jnp.int32, (2 * c, 2 * c), 0)
    ci = lax.broadcasted_iota(jnp.int32, (2 * c, 2 * c), 1)
    rt, cs = ri % c, ci % c
    before = (rt - cs) * (1 - 2 * d) > 0
    keep = before | ((ri >= c) & (cs == rt))
    rt_c = lax.broadcasted_iota(jnp.int32, (c, c), 0)
    cs_c = lax.broadcasted_iota(jnp.int32, (c, c), 1)
    tri = jnp.where((rt_c - cs_c) * (1 - 2 * d) >= 0, 1.0, 0.0).astype(BF16)
    eye = ri == ci
    lane_c = lax.broadcasted_iota(jnp.int32, (c, LANES), 1)
    lo_half = lane_c < RWKV_HEAD
    zeros_c = jnp.zeros((c, LANES), F32)
    nsub = rows // c

    def head_slab(x, h):
        s = x[:, (h // 2) * LANES:(h // 2 + 1) * LANES]
        if h % 2:
            s = pltpu.roll(s, RWKV_HEAD, 1)
        return jnp.where(lo_half[0:x.shape[0]], s, 0.0)

    chains = [(s, h) for s in range(nsub) for h in range(RWKV_HEADS)]
    ah, rh, vh, w_bot, pw, x, f, e_hat, g_hat = {}, {}, {}, {}, {}, {}, {}, {}, {}
    for s in range(nsub):
        sl = slice(s * c, (s + 1) * c)
        lw = lw_sc[sl, :]
        cum = _mm_exact_lhs(tri, lw)
        tot = jnp.sum(lw, axis=0, keepdims=True)
        kn_c, a_c, kd_c = kn_sc[sl, :], a_sc[sl, :], kd_sc[sl, :]
        e_neg = jnp.exp(-cum)
        e_end = jnp.exp(tot - cum)
        r_t = r_sc[sl, :] * jnp.exp(cum)
        a_t = -kn_c * jnp.exp(cum - lw)
        b_t = kn_c * a_c * e_neg
        k_t = kd_c * e_neg
        b_e = kn_c * a_c * e_end
        k_e = kd_c * e_end
        g_end = jnp.exp(tot)
        v_c = v_sc[sl, :]
        for h in range(RWKV_HEADS):
            ah[s, h], rh[s, h], vh[s, h] = head_slab(a_t, h), head_slab(r_t, h), head_slab(v_c, h)
            g = _mm_nt(jnp.concatenate([ah[s, h], rh[s, h]], 0),
                       jnp.concatenate([head_slab(b_t, h), head_slab(k_t, h)], 0))
            g = jnp.where(keep, g, 0.0)
            w_bot[s, h] = g[c:2 * c]
            pw[s, h] = jnp.where(lo_half, g[0:c], 0.0)
            x[s, h] = jnp.where(lo_half, 0.0, g[0:c])
            e_hat[s, h] = jnp.concatenate([head_slab(b_e, h), head_slab(k_e, h)], 0)
            g_hat[s, h] = head_slab(g_end, h)
    for ch in chains:
        x[ch] = jnp.concatenate([ah[ch], _mm(x[ch], jnp.concatenate([zeros_c, vh[ch]], 0))], 1)
    zeros_x = jnp.zeros((c, 2 * LANES), F32)
    for step in range(6):
        for ch in chains:
            x[ch] = x[ch] + _mm(pw[ch], jnp.concatenate([x[ch], zeros_x], 0))
        if step < 5:
            for ch in chains:
                pw[ch] = _mm(pw[ch], jnp.concatenate([pw[ch], zeros_c], 0))
    for ch in chains:
        f[ch] = jnp.concatenate([x[ch], jnp.concatenate([zeros_c, vh[ch]], 1)], 0)
        wf = _mm(w_bot[ch], f[ch])
        rp_sc[ch[0], ch[1]] = rh[ch] + wf[:, 0:LANES]
        yl_sc[ch[0], ch[1]] = wf[:, LANES:2 * LANES]
    for ch in chains:
        mn = _mm_tn(e_hat[ch], f[ch])
        mt_sc[ch[0], ch[1]] = (mn[:, 0:LANES] + jnp.where(eye, g_hat[ch], 0.0))[0:c]
        nt_sc[ch[0], ch[1]] = mn[0:c, LANES:2 * LANES]

    def advance(s, carry):
        sub = s + d * (nsub - 1 - 2 * s)
        sl = pl.ds(pl.multiple_of(sub * c, c), c)
        for j in range(RWKV_HEADS // 2):
            y_pair = None
            for h in (2 * j, 2 * j + 1):
                hs = h_sc[h]
                y_h = _mm(rp_sc[sub, h], hs) + yl_sc[sub, h]
                h_new = _mm3(mt_sc[sub, h], hs) + nt_sc[sub, h]
                h_sc[h] = jnp.concatenate([h_new, zeros_c], 0)
                y_pair = y_h if y_pair is None else y_pair + pltpu.roll(y_h, RWKV_HEAD, 1)
            y_ref[0, 0, sl, j * LANES:(j + 1) * LANES] = y_pair
        return carry

    lax.fori_loop(0, nsub, advance, 0)


def _rwkv_scan(p_rwkv, mu, w_lora, w0, a0, k_k, k_a, rows):
    bsz, t, _ = p_rwkv.shape
    nblk = t // rows
    w = RWKV_WIDTH
    blk_of = lambda d, b, i: i + d * (nblk - 1 - 2 * i)
    halo_prev, halo_next = _halo_specs(rows, t, blk_of)
    vec = lambda n: pl.BlockSpec((1, n), lambda d, b, i: (0, 0))
    dvec = lambda n: pl.BlockSpec((1, 1, n), lambda d, b, i: (d, 0, 0))
    blocks = (_nbytes((rows + 16, RWKV_IN), F32) + _nbytes((LANES, 2 * w), F32) + _nbytes((rows, w), F32))
    per_chain = (rows // CHUNK, RWKV_HEADS, CHUNK, LANES)
    scratch = 6 * _nbytes((rows, w), F32) + _nbytes((RWKV_HEADS, LANES, LANES), F32) + 4 * _nbytes(per_chain, F32)
    return pl.pallas_call(
        functools.partial(_rwkv_scan_kernel, rows=rows),
        grid=(2, bsz, nblk),
        in_specs=[
            pl.BlockSpec((1, rows, RWKV_IN), lambda d, b, i: (b, blk_of(d, b, i), 0)),
            halo_prev, halo_next,
            vec(RWKV_IN),
            pl.BlockSpec((1, LANES, 2 * w), lambda d, b, i: (d, 0, 0)),
            dvec(w), dvec(w), vec(w), vec(w),
        ],
        out_specs=pl.BlockSpec((1, 1, rows, w), lambda d, b, i: (d, b, blk_of(d, b, i), 0)),
        out_shape=jax.ShapeDtypeStruct((2, bsz, t, w), F32),
        scratch_shapes=([pltpu.VMEM((rows, w), F32)] * 6 + [pltpu.VMEM((RWKV_HEADS, LANES, LANES), F32)]
                        + [pltpu.VMEM(per_chain, F32)] * 4),
        compiler_params=pltpu.CompilerParams(
            dimension_semantics=("parallel", "parallel", "arbitrary"),
            vmem_limit_bytes=_vmem_limit(blocks, scratch + 8 * _nbytes((rows, RWKV_IN), F32))),
        name="rwkv_scan",
    )(p_rwkv, p_rwkv, p_rwkv, mu, w_lora, w0, a0, k_k, k_a)


def _mix_kernel(x_ref, mod_ref, ot_ref, y_ref, pr_ref, hp_ref, hn_ref, pg_ref, mu_ref, wi_ref, a0_ref, ka_ref,
                rk_ref, wg_ref, lw_ref, lb_ref, wmo_ref, wro_ref, wo_ref, o_ref):
    i = pl.program_id(1)
    w = RWKV_WIDTH
    prev_row, next_row = _halo_rows(hp_ref, hn_ref, i, pl.num_programs(1))
    xs = _shift_mix(pr_ref[0], prev_row, next_row, mu_ref[...])
    r, k, v = xs[:, 0:w], xs[:, w:2 * w], xs[:, 2 * w:3 * w]
    pre = _mm3(xs[:, LORA_OFF:LORA_OFF + LANES], wi_ref[...])
    ka = ka_ref[...]
    bonus_k = (k * (1.0 + (jax.nn.sigmoid(a0_ref[0] + pre[:, 0:w]) - 1.0) * ka)
               + k * (1.0 + (jax.nn.sigmoid(a0_ref[1] + pre[:, w:2 * w]) - 1.0) * ka))
    gate = _mm(jax.nn.sigmoid(xs[:, GATE_OFF:GATE_OFF + GATE_LORA]), wg_ref[...])
    ones_h = _head_ones(w, RWKV_HEAD)
    inv_n = 1.0 / RWKV_HEAD
    y = y_ref[0, 0] + y_ref[1, 0]
    mean = _mm_exact_rhs(y, ones_h) * inv_n
    yc = y - mean
    var = _mm_exact_rhs(yc * yc, ones_h) * inv_n
    yn = yc * lax.rsqrt(var + LNX_EPS) * lw_ref[...] + lb_ref[...]
    bonus = _mm_exact_rhs(r * bonus_k * rk_ref[...], ones_h) * v
    o_rwkv = _mm((yn + bonus) * gate, wro_ref[...])
    o_mla = _mm_tn(ot_ref[0], wmo_ref[...])
    pg = pg_ref[0].astype(F32)
    merged = jax.nn.sigmoid(pg[:, 0:D_MODEL]) * o_mla + jax.nn.sigmoid(pg[:, D_MODEL:2 * D_MODEL]) * o_rwkv
    g1 = mod_ref[0, :, 2 * D_MODEL:3 * D_MODEL]
    o_ref[0] = x_ref[0] + g1 * _mm(merged, wo_ref[...])


def _mix_out(x, mod3, o_t, y, p_rwkv, p_gate, mu, w_iclr, a0, k_a, r_k, w_gate, lnx_w, lnx_b,
             w_mla_o, w_rwkv_o, w_out, tm):
    bsz, t, _ = x.shape
    w = RWKV_WIDTH
    halo_prev, halo_next = _halo_specs(tm, t, lambda b, i: i)
    const2 = lambda b, i: (0, 0)
    vec = lambda n: pl.BlockSpec((1, n), const2)
    blocks = (2 * _nbytes((tm, D_MODEL), F32) + _nbytes((MLA_WIDTH, tm), BF16) + 2 * _nbytes((tm, w), F32)
              + _nbytes((tm + 16, RWKV_IN), F32) + _nbytes((tm, GATE_IN), BF16)
              + _nbytes((LANES, 2 * w), F32) + _nbytes((GATE_LORA, w), BF16)
              + 2 * _nbytes((w, D_MODEL), BF16) + _nbytes((D_MODEL, D_MODEL), BF16))
    return pl.pallas_call(
        _mix_kernel,
        grid=(bsz, t // tm),
        in_specs=[
            pl.BlockSpec((1, tm, D_MODEL), lambda b, i: (b, i, 0)),
            pl.BlockSpec((1, 1, 6 * D_MODEL), lambda b, i: (b, 0, 0)),
            pl.BlockSpec((1, MLA_WIDTH, tm), lambda b, i: (b, 0, i)),
            pl.BlockSpec((2, 1, tm, w), lambda b, i: (0, b, i, 0)),
            pl.BlockSpec((1, tm, RWKV_IN), lambda b, i: (b, i, 0)),
            halo_prev, halo_next,
            pl.BlockSpec((1, tm, GATE_IN), lambda b, i: (b, i, 0)),
            vec(RWKV_IN),
            pl.BlockSpec((LANES, 2 * w), const2),
            pl.BlockSpec((2, 1, w), lambda b, i: (0, 0, 0)),
            vec(w), vec(w),
            pl.BlockSpec((GATE_LORA, w), const2),
            vec(w), vec(w),
            pl.BlockSpec((MLA_WIDTH, D_MODEL), const2),
            pl.BlockSpec((w, D_MODEL), const2),
            pl.BlockSpec((D_MODEL, D_MODEL), const2),
        ],
        out_specs=pl.BlockSpec((1, tm, D_MODEL), lambda b, i: (b, i, 0)),
        out_shape=jax.ShapeDtypeStruct((bsz, t, D_MODEL), F32),
        compiler_params=pltpu.CompilerParams(
            dimension_semantics=("parallel", "parallel"),
            vmem_limit_bytes=_vmem_limit(blocks, 10 * _nbytes((tm, RWKV_IN), F32))),
        name="mix_out",
    )(x, mod3, o_t, y, p_rwkv, p_rwkv, p_rwkv, p_gate, mu, w_iclr, a0, k_a, r_k, w_gate, lnx_w, lnx_b,
      w_mla_o, w_rwkv_o, w_out)


def _ffn_kernel(x_ref, mod_ref, nw_ref, wi_ref, wo_ref, fn_ref, o_ref, acc_ref, *, tf):
    x = x_ref[0]
    sh = mod_ref[0, :, 3 * D_MODEL:4 * D_MODEL]
    sc = mod_ref[0, :, 4 * D_MODEL:5 * D_MODEL]
    g2 = mod_ref[0, :, 5 * D_MODEL:6 * D_MODEL]
    h = (_rmsnorm(x, nw_ref[...]) * (1.0 + sc) + sh).astype(BF16)
    for j in range(D_FF // tf):
        u = _mm(h, wi_ref[:, j * tf:(j + 1) * tf])
        zg = _mm(h, wi_ref[:, D_FF + j * tf:D_FF + (j + 1) * tf])
        part = _mm(u * jax.nn.sigmoid(u) * zg, wo_ref[j * tf:(j + 1) * tf, :])
        if j == 0:
            acc_ref[...] = part
        else:
            acc_ref[...] += part
    o_ref[0] = _rmsnorm(x + g2 * acc_ref[...], fn_ref[...])


def _ffn(x, mod3, norm_w, w_in, w_out, final_w, tm, tf):
    bsz, t, _ = x.shape
    blocks = (2 * _nbytes((tm, D_MODEL), F32) + _nbytes((D_MODEL, 2 * D_FF), BF16)
              + _nbytes((D_FF, D_MODEL), BF16))
    const2 = lambda b, i: (0, 0)
    return pl.pallas_call(
        functools.partial(_ffn_kernel, tf=tf),
        grid=(bsz, t // tm),
        in_specs=[
            pl.BlockSpec((1, tm, D_MODEL), lambda b, i: (b, i, 0)),
            pl.BlockSpec((1, 1, 6 * D_MODEL), lambda b, i: (b, 0, 0)),
            pl.BlockSpec((1, D_MODEL), const2),
            pl.BlockSpec((D_MODEL, 2 * D_FF), const2),
            pl.BlockSpec((D_FF, D_MODEL), const2),
            pl.BlockSpec((1, D_MODEL), const2),
        ],
        out_specs=pl.BlockSpec((1, tm, D_MODEL), lambda b, i: (b, i, 0)),
        out_shape=jax.ShapeDtypeStruct((bsz, t, D_MODEL), F32),
        scratch_shapes=[pltpu.VMEM((tm, D_MODEL), F32)],
        compiler_params=pltpu.CompilerParams(
            dimension_semantics=("parallel", "parallel"),
            vmem_limit_bytes=_vmem_limit(blocks, _nbytes((tm, D_MODEL), F32) + 6 * _nbytes((tm, tf), F32))),
        name="ffn",
    )(x, mod3, norm_w, w_in, w_out, final_w)


def _rope_tables(t):
    half = QK_ROPE // 2
    inv = ROPE_BASE ** (-jnp.arange(half, dtype=F32) / half)
    ang = jnp.arange(t, dtype=F32)[:, None] * inv[None, :]
    cos, sin = jnp.cos(ang), jnp.sin(ang)
    z = lambda n: jnp.zeros((t, n), F32)
    rc = jnp.concatenate([jnp.ones((t, QK_NOPE), F32), cos, cos, z(LANES - QK_NOPE - QK_ROPE)], 1)
    rs1 = jnp.concatenate([z(QK_NOPE + half), sin, z(LANES - QK_NOPE - QK_ROPE)], 1)
    rs2 = jnp.concatenate([z(QK_NOPE), -sin, z(LANES - QK_NOPE - half)], 1)
    return rc, rs1, rs2


def _prepare_params(w_in, w_uq, w_ukv, w_decay_up, w_iclr_up, w_gate_up, w_mla_o, w_rwkv_o, w_out,
                    w_ffn_in, w_ffn_out):
    dqk = QK_NOPE + QK_ROPE
    kpe_tile = jnp.zeros((D_MODEL, LANES), F32).at[:, QK_NOPE:QK_NOPE + QK_ROPE].set(
        w_in[:, Q_LORA + KV_LORA:MLA_IN])
    w_in_p = jnp.concatenate([w_in[:, 0:Q_LORA + KV_LORA], kpe_tile, w_in[:, MLA_IN:]], 1).astype(BF16)
    wq = w_uq.reshape(Q_LORA, MLA_HEADS, dqk)
    wq_p = jnp.pad(wq, ((0, 0), (0, 0), (0, HEAD_PAD - dqk))).reshape(Q_LORA, MLA_HEADS * HEAD_PAD).astype(BF16)
    wkv = w_ukv.reshape(KV_LORA, MLA_HEADS, QK_NOPE + V_HEAD)
    wk_p = jnp.pad(wkv[:, :, 0:QK_NOPE], ((0, 0), (0, 0), (0, HEAD_PAD - QK_NOPE))).reshape(
        KV_LORA, MLA_HEADS * HEAD_PAD).astype(BF16)
    wvt = wkv[:, :, QK_NOPE:].reshape(KV_LORA, MLA_WIDTH).T.astype(BF16)
    zl = jnp.zeros((DECAY_LORA, RWKV_WIDTH), F32)
    w_lora = jnp.stack([jnp.concatenate([jnp.concatenate([w_decay_up[d], zl], 1),
                                         jnp.concatenate([zl, w_iclr_up[d]], 1)], 0) for d in range(2)])
    w_iclr = jnp.concatenate([jnp.zeros((DECAY_LORA, 2 * RWKV_WIDTH), F32),
                              jnp.concatenate([w_iclr_up[0], w_iclr_up[1]], 1)], 0)
    return dict(w_in_p=w_in_p, wq_p=wq_p, wk_p=wk_p, wvt=wvt, w_lora=w_lora, w_iclr=w_iclr,
                w_gate=w_gate_up.astype(BF16), w_mla_o=w_mla_o.astype(BF16), w_rwkv_o=w_rwkv_o.astype(BF16),
                w_out=w_out.astype(BF16), w_ffn_in=w_ffn_in.astype(BF16), w_ffn_out=w_ffn_out.astype(BF16))


def _tiles(t):
    return dict(tm_proj=min(256, t), tm_prep=min(512, t), tq=min(2048, t), tk=min(512, t),
                rows_scan=min(256, t), tm_mix=min(256, t), tm_ffn=min(256, t), tf=256)


def _encoder(x, mod, pp, norm_mix, q_a_norm, kv_a_norm, mu_shift, w0, a0, k_k, k_a, r_k, lnx_w, lnx_b,
             norm_ffn, final_norm):
    bsz, t, _ = x.shape
    ts = _tiles(t)
    row = lambda v: v.reshape(1, -1)
    mod3 = mod.reshape(bsz, 1, 6 * D_MODEL)
    p_mla, p_rwkv, p_gate = _in_proj(x, mod3, row(norm_mix), pp["w_in_p"], ts["tm_proj"])
    q, k, vt = _mla_prep(p_mla, row(q_a_norm), row(kv_a_norm), pp["wq_p"], pp["wk_p"], pp["wvt"],
                         *_rope_tables(t), ts["tm_prep"], ts["tk"])
    o_t = _attention(q, k, vt, ts["tq"])
    y = _rwkv_scan(p_rwkv, row(mu_shift), pp["w_lora"], w0.reshape(2, 1, -1), a0.reshape(2, 1, -1),
                   row(k_k), row(k_a), ts["rows_scan"])
    x1 = _mix_out(x, mod3, o_t, y, p_rwkv, p_gate, row(mu_shift), pp["w_iclr"], a0.reshape(2, 1, -1), row(k_a),
                  row(r_k), pp["w_gate"], row(lnx_w), row(lnx_b), pp["w_mla_o"], pp["w_rwkv_o"], pp["w_out"],
                  ts["tm_mix"])
    return _ffn(x1, mod3, row(norm_ffn), pp["w_ffn_in"], pp["w_ffn_out"], row(final_norm), ts["tm_ffn"], ts["tf"])


def kernel(x_prompt, x_sample, c_prompt, c_sample, w_ada, b_ada, norm_mix, w_in, q_a_norm, kv_a_norm, w_uq, w_ukv, mu_shift, w0, w_decay_up, a0, w_iclr_up, w_gate_up, k_k, k_a, r_k, lnx_w, lnx_b, w_mla_o, w_rwkv_o, w_out, norm_ffn, w_ffn_in, w_ffn_out, final_norm):
    pp = _prepare_params(w_in[0], w_uq[0], w_ukv[0], w_decay_up[0], w_iclr_up[0], w_gate_up[0], w_mla_o[0],
                         w_rwkv_o[0], w_out[0], w_ffn_in[0], w_ffn_out[0])
    nb = x_prompt.shape[0]
    mod = _adaln_mod(jnp.concatenate([c_prompt, c_sample], 0), w_ada[0], b_ada[0])
    args = (pp, norm_mix[0], q_a_norm[0], kv_a_norm[0], mu_shift[0], w0[0], a0[0], k_k[0], k_a[0], r_k[0],
            lnx_w[0], lnx_b[0], norm_ffn[0], final_norm)
    return (_encoder(x_prompt, mod[:nb], *args), _encoder(x_sample, mod[nb:], *args))
```

```python
import functools

import jax
import jax.numpy as jnp
from jax import lax
from jax.experimental import pallas as pl
from jax.experimental.pallas import tpu as pltpu

F32 = jnp.float32
BF16 = jnp.bfloat16

D_MODEL = 1024
MLA_HEADS = 8
QK_NOPE = 64
QK_ROPE = 32
V_HEAD = 64
V_ROWS = 80
Q_LORA = 384
KV_LORA = 256
MLA_WIDTH = MLA_HEADS * V_HEAD
ROPE_BASE = 10000.0
RWKV_HEADS = 8
RWKV_HEAD = 64
RWKV_WIDTH = RWKV_HEADS * RWKV_HEAD
DECAY_LORA = 64
ICLR_LORA = 64
GATE_LORA = 128
D_FF = 2816
EPS = 1e-6
LNX_EPS = 64e-5
MLA_IN = Q_LORA + KV_LORA + QK_ROPE
RWKV_IN = 3 * RWKV_WIDTH + DECAY_LORA + ICLR_LORA + GATE_LORA
GATE_IN = 2 * D_MODEL

LANES = 128
HEAD_PAD = LANES
MLA_IN_PAD = Q_LORA + KV_LORA + LANES
IN_COLS_PAD = MLA_IN_PAD + RWKV_IN + GATE_IN
LORA_OFF = 3 * RWKV_WIDTH
GATE_OFF = LORA_OFF + DECAY_LORA + ICLR_LORA
VMEM_PHYS_BYTES = 64 * 1024 * 1024
VMEM_CAP_BYTES = 60000 * 1024
LOG2E = 1.4426950408889634
CHUNK = 64


def _vmem_limit(block_bytes, temp_bytes):
    return int(min(2 * block_bytes + temp_bytes, VMEM_CAP_BYTES))


def _nbytes(shape, dtype):
    n = 1
    for s in shape:
        n *= s
    return n * jnp.dtype(dtype).itemsize


def _bf(x):
    return x if x.dtype == BF16 else x.astype(BF16)


def _mm(a, b):
    return jnp.dot(_bf(a), _bf(b), preferred_element_type=F32)


def _mm_nt(a, b):
    return lax.dot_general(_bf(a), _bf(b), (((1,), (1,)), ((), ())), preferred_element_type=F32)


def _mm_tn(a, b):
    return lax.dot_general(_bf(a), _bf(b), (((0,), (0,)), ((), ())), preferred_element_type=F32)


def _split2(x):
    hi = x.astype(BF16)
    lo = (x - hi.astype(F32)).astype(BF16)
    return hi, lo


def _mm3(a, b):
    ah, al = _split2(a)
    bh, bl = _split2(b)
    return _mm(ah, bh) + (_mm(ah, bl) + _mm(al, bh))


def _mm_exact_lhs(a_bf, b):
    b0, b1 = _split2(b)
    return _mm(a_bf, b0) + _mm(a_bf, b1)


def _mm_exact_rhs(a, b_bf):
    a0, a1 = _split2(a)
    return _mm(a0, b_bf) + _mm(a1, b_bf)


def _head_ones(n, head):
    ri = lax.broadcasted_iota(jnp.int32, (n, n), 0) // head
    ci = lax.broadcasted_iota(jnp.int32, (n, n), 1) // head
    return jnp.where(ri == ci, 1.0, 0.0).astype(BF16)


def _rmsnorm(x, g):
    return x * lax.rsqrt(jnp.mean(x * x, axis=-1, keepdims=True) + EPS) * g


def _softplus(x):
    return jnp.maximum(x, 0.0) + jnp.log(1.0 + jnp.exp(-jnp.abs(x)))


def _mod_kernel(c_ref, w_ref, b_ref, o_ref):
    c = c_ref[...]
    o_ref[...] = _mm3(c * jax.nn.sigmoid(c), w_ref[...]) + b_ref[...]


def _adaln_mod(c_all, w_ada, b_ada):
    rows, n = c_all.shape[0], w_ada.shape[1]
    tn = 1536
    blocks = _nbytes((rows, D_MODEL), F32) + _nbytes((D_MODEL, tn), F32) + _nbytes((rows + 1, tn), F32)
    return pl.pallas_call(
        _mod_kernel,
        grid=(n // tn,),
        in_specs=[
            pl.BlockSpec((rows, D_MODEL), lambda j: (0, 0)),
            pl.BlockSpec((D_MODEL, tn), lambda j: (0, j)),
            pl.BlockSpec((1, tn), lambda j: (0, j)),
        ],
        out_specs=pl.BlockSpec((rows, tn), lambda j: (0, j)),
        out_shape=jax.ShapeDtypeStruct((rows, n), F32),
        compiler_params=pltpu.CompilerParams(
            dimension_semantics=("arbitrary",),
            vmem_limit_bytes=_vmem_limit(blocks, 3 * _nbytes((D_MODEL, tn), F32))),
        name="adaln_mod",
    )(c_all, w_ada, b_ada.reshape(1, n))


def _inproj_kernel(x_ref, mod_ref, nw_ref, w_ref, pm_ref, pr_ref, pg_ref):
    sh = mod_ref[0, :, 0:D_MODEL]
    sc = mod_ref[0, :, D_MODEL:2 * D_MODEL]
    h = (_rmsnorm(x_ref[0], nw_ref[...]) * (1.0 + sc) + sh).astype(BF16)
    a, b = MLA_IN_PAD, MLA_IN_PAD + RWKV_IN
    pm_ref[0] = _mm(h, w_ref[:, 0:a]).astype(BF16)
    pr_ref[0] = _mm(h, w_ref[:, a:b])
    pg_ref[0] = _mm(h, w_ref[:, b:IN_COLS_PAD]).astype(BF16)


def _in_proj(x, mod3, norm_w, w_in_p, tm):
    bsz, t, _ = x.shape
    blocks = (_nbytes((tm, D_MODEL), F32) + _nbytes((D_MODEL, IN_COLS_PAD), BF16)
              + _nbytes((tm, MLA_IN_PAD), BF16) + _nbytes((tm, RWKV_IN), F32) + _nbytes((tm, GATE_IN), BF16))
    return pl.pallas_call(
        _inproj_kernel,
        grid=(bsz, t // tm),
        in_specs=[
            pl.BlockSpec((1, tm, D_MODEL), lambda b, i: (b, i, 0)),
            pl.BlockSpec((1, 1, 6 * D_MODEL), lambda b, i: (b, 0, 0)),
            pl.BlockSpec((1, D_MODEL), lambda b, i: (0, 0)),
            pl.BlockSpec((D_MODEL, IN_COLS_PAD), lambda b, i: (0, 0)),
        ],
        out_specs=[
            pl.BlockSpec((1, tm, MLA_IN_PAD), lambda b, i: (b, i, 0)),
            pl.BlockSpec((1, tm, RWKV_IN), lambda b, i: (b, i, 0)),
            pl.BlockSpec((1, tm, GATE_IN), lambda b, i: (b, i, 0)),
        ],
        out_shape=[
            jax.ShapeDtypeStruct((bsz, t, MLA_IN_PAD), BF16),
            jax.ShapeDtypeStruct((bsz, t, RWKV_IN), F32),
            jax.ShapeDtypeStruct((bsz, t, GATE_IN), BF16),
        ],
        compiler_params=pltpu.CompilerParams(
            dimension_semantics=("parallel", "parallel"),
            vmem_limit_bytes=_vmem_limit(blocks, 2 * _nbytes((tm, IN_COLS_PAD), F32))),
        name="in_proj",
    )(x, mod3, norm_w, w_in_p)


def _mla_prep_kernel(pm_ref, qn_ref, kvn_ref, wq_ref, wk_ref, wvt_ref, rc_ref, rs1_ref, rs2_ref,
                     q_ref, k_ref, vt_ref, *, tk):
    p = pm_ref[0].astype(F32)
    cq = _rmsnorm(p[:, 0:Q_LORA], qn_ref[...]).astype(BF16)
    ckv = _rmsnorm(p[:, Q_LORA:Q_LORA + KV_LORA], kvn_ref[...]).astype(BF16)
    rc, rs1, rs2 = rc_ref[...], rs1_ref[...], rs2_ref[...]

    def rope(x):
        return x * rc + pltpu.roll(x, QK_ROPE // 2, 1) * rs1 + pltpu.roll(x, LANES - QK_ROPE // 2, 1) * rs2

    k_pe = rope(p[:, Q_LORA + KV_LORA:MLA_IN_PAD])
    q = _mm(cq, wq_ref[...])
    kn = _mm(ckv, wk_ref[...])
    vt = _mm_nt(wvt_ref[...], ckv)
    scale = (QK_NOPE + QK_ROPE) ** -0.5 * LOG2E
    tail = jnp.where(lax.broadcasted_iota(jnp.int32, (V_ROWS - V_HEAD, tk), 0) == 0, 1.0, 0.0)
    for h in range(MLA_HEADS):
        sl = slice(h * HEAD_PAD, (h + 1) * HEAD_PAD)
        q_ref[0, h] = (rope(q[:, sl]) * scale).astype(BF16)
        k_ref[0, h] = (kn[:, sl] + k_pe).astype(BF16)
        for c in range(vt.shape[1] // tk):
            vt_ref[0, h, c] = jnp.concatenate(
                [vt[h * V_HEAD:(h + 1) * V_HEAD, c * tk:(c + 1) * tk], tail], 0).astype(BF16)


def _mla_prep(p_mla, q_norm, kv_norm, wq_p, wk_p, wvt, rope_c, rope_s1, rope_s2, tm, tk):
    bsz, t, _ = p_mla.shape
    assert tm % tk == 0
    hp = MLA_HEADS * HEAD_PAD
    blocks = (_nbytes((tm, MLA_IN_PAD), BF16) + _nbytes((Q_LORA + KV_LORA, hp), BF16)
              + _nbytes((MLA_WIDTH, KV_LORA), BF16) + 3 * _nbytes((tm, LANES), F32)
              + 2 * _nbytes((tm, hp), BF16) + _nbytes((MLA_WIDTH, tm), BF16))
    const = lambda b, i: (0, 0)
    return pl.pallas_call(
        functools.partial(_mla_prep_kernel, tk=tk),
        grid=(bsz, t // tm),
        in_specs=[
            pl.BlockSpec((1, tm, MLA_IN_PAD), lambda b, i: (b, i, 0)),
            pl.BlockSpec((1, Q_LORA), const),
            pl.BlockSpec((1, KV_LORA), const),
            pl.BlockSpec((Q_LORA, hp), const),
            pl.BlockSpec((KV_LORA, hp), const),
            pl.BlockSpec((MLA_WIDTH, KV_LORA), const),
            pl.BlockSpec((tm, LANES), lambda b, i: (i, 0)),
            pl.BlockSpec((tm, LANES), lambda b, i: (i, 0)),
            pl.BlockSpec((tm, LANES), lambda b, i: (i, 0)),
        ],
        out_specs=[
            pl.BlockSpec((1, MLA_HEADS, tm, HEAD_PAD), lambda b, i: (b, 0, i, 0)),
            pl.BlockSpec((1, MLA_HEADS, tm, HEAD_PAD), lambda b, i: (b, 0, i, 0)),
            pl.BlockSpec((1, MLA_HEADS, tm // tk, V_ROWS, tk), lambda b, i: (b, 0, i, 0, 0)),
        ],
        out_shape=[
            jax.ShapeDtypeStruct((bsz, MLA_HEADS, t, HEAD_PAD), BF16),
            jax.ShapeDtypeStruct((bsz, MLA_HEADS, t, HEAD_PAD), BF16),
            jax.ShapeDtypeStruct((bsz, MLA_HEADS, t // tk, V_ROWS, tk), BF16),
        ],
        compiler_params=pltpu.CompilerParams(
            dimension_semantics=("parallel", "parallel"),
            vmem_limit_bytes=_vmem_limit(blocks, 6 * _nbytes((tm, hp), F32))),
        name="mla_prep",
    )(p_mla, q_norm, kv_norm, wq_p, wk_p, wvt, rope_c, rope_s1, rope_s2)


def _attn_kernel(q_ref, k_ref, vt_ref, o_ref):
    q = q_ref[0, 0]
    tq = q.shape[0]
    nk, tk = vt_ref.shape[2], vt_ref.shape[4]

    def body(j, carry):
        m, acc = carry
        r0 = pl.multiple_of(j * tk, tk)
        st = _mm_nt(k_ref[0, 0, pl.ds(r0, tk), :], q)
        m_new = jnp.maximum(m, jnp.max(st, axis=0, keepdims=True))
        p = jnp.exp2(st - m_new)
        acc = jnp.exp2(m - m_new) * acc + _mm(vt_ref[0, 0, j], p)
        return m_new, acc

    init = (jnp.full((1, tq), -jnp.inf, F32), jnp.zeros((V_ROWS, tq), F32))
    _, acc = lax.fori_loop(0, nk, body, init)
    o_ref[0] = (acc[0:V_HEAD] / acc[V_HEAD:V_HEAD + 1]).astype(BF16)


def _attention(q, k, vt, tq):
    bsz, nh, t, _ = q.shape
    nk, tk = vt.shape[2], vt.shape[4]
    blocks = (_nbytes((tq, HEAD_PAD), BF16) + _nbytes((t, HEAD_PAD), BF16) + _nbytes((V_ROWS, t), BF16)
              + _nbytes((V_HEAD, tq), BF16))
    return pl.pallas_call(
        _attn_kernel,
        grid=(bsz, nh, t // tq),
        in_specs=[
            pl.BlockSpec((1, 1, tq, HEAD_PAD), lambda b, h, i: (b, h, i, 0)),
            pl.BlockSpec((1, 1, t, HEAD_PAD), lambda b, h, i: (b, h, 0, 0)),
            pl.BlockSpec((1, 1, nk, V_ROWS, tk), lambda b, h, i: (b, h, 0, 0, 0)),
        ],
        out_specs=pl.BlockSpec((1, V_HEAD, tq), lambda b, h, i: (b, h, i)),
        out_shape=jax.ShapeDtypeStruct((bsz, nh * V_HEAD, t), BF16),
        compiler_params=pltpu.CompilerParams(
            dimension_semantics=("parallel", "parallel", "arbitrary"),
            vmem_limit_bytes=_vmem_limit(blocks, 6 * _nbytes((tk, tq), F32))),
        name="mla_attn",
    )(q, k, vt)


def _shift_mix(p, prev_row, next_row, mu):
    n = p.shape[0]
    rid = lax.broadcasted_iota(jnp.int32, (n, 1), 0)
    up = jnp.where(rid == 0, prev_row, pltpu.roll(p, 1, 0))
    dn = jnp.where(rid == n - 1, next_row, pltpu.roll(p, n - 1, 0))
    return p + mu * (0.5 * (up + dn) - p)


def _halo_rows(hp_ref, hn_ref, blk, nblk):
    prev_row = jnp.where(blk > 0, hp_ref[0, 7:8, :], 0.0)
    next_row = jnp.where(blk < nblk - 1, hn_ref[0, 0:1, :], 0.0)
    return prev_row, next_row


def _halo_specs(rows, t, blk_of):
    per = rows // 8
    last = t // 8 - 1
    prev = lambda *g: (g[-2], jnp.maximum(blk_of(*g) * per - 1, 0), 0)
    nxt = lambda *g: (g[-2], jnp.minimum((blk_of(*g) + 1) * per, last), 0)
    return pl.BlockSpec((1, 8, RWKV_IN), prev), pl.BlockSpec((1, 8, RWKV_IN), nxt)


def _rwkv_scan_kernel(pr_ref, hp_ref, hn_ref, mu_ref, wl_ref, w0_ref, a0_ref, kk_ref, ka_ref, y_ref,
                      r_sc, v_sc, kn_sc, lw_sc, a_sc, kd_sc, h_sc, lhs_sc, yl_sc, nt_sc, *, rows):
    d = pl.program_id(0)
    i = pl.program_id(2)
    nblk = pl.num_programs(2)
    blk = i + d * (nblk - 1 - 2 * i)
    c = CHUNK
    w = RWKV_WIDTH

    @pl.when(i == 0)
    def _():
        h_sc[...] = jnp.zeros_like(h_sc)

    prev_row, next_row = _halo_rows(hp_ref, hn_ref, blk, nblk)
    xs = _shift_mix(pr_ref[0], prev_row, next_row, mu_ref[...])
    k = xs[:, w:2 * w]
    z = xs[:, LORA_OFF:LORA_OFF + LANES]
    lane = lax.broadcasted_iota(jnp.int32, z.shape, 1)
    pre = _mm3(jnp.where(lane < DECAY_LORA, jnp.tanh(z), z), wl_ref[0])
    wlog = -_softplus(-(w0_ref[0] + pre[:, 0:w])) - 0.5
    a = jax.nn.sigmoid(a0_ref[0] + pre[:, w:2 * w])
    kn = k * kk_ref[...]
    ones_h = _head_ones(w, RWKV_HEAD)
    kn = kn * lax.rsqrt(_mm_exact_rhs(kn * kn, ones_h) + 1e-12)
    r_sc[...] = xs[:, 0:w]
    v_sc[...] = xs[:, 2 * w:3 * w]
    kn_sc[...] = kn
    lw_sc[...] = -jnp.exp(wlog)
    a_sc[...] = a
    kd_sc[...] = k * (1.0 + (a - 1.0) * ka_ref[...])

    ri = lax.broadcasted_iota(jnp.int32, (2 * c, 2 * c), 0)
    ci = lax.broadcasted_iota(jnp.int32, (2 * c, 2 * c), 1)
    rt, cs = ri % c, ci % c
    before = (rt - cs) * (1 - 2 * d) > 0
    keep = before | ((ri >= c) & (cs == rt))
    rt_c = lax.broadcasted_iota(jnp.int32, (c, c), 0)
    cs_c = lax.broadcasted_iota(jnp.int32, (c, c), 1)
    tri = jnp.where((rt_c - cs_c) * (1 - 2 * d) >= 0, 1.0, 0.0).astype(BF16)
    eye = ri == ci
    lane_c = lax.broadcasted_iota(jnp.int32, (c, LANES), 1)
    lo_half = lane_c < RWKV_HEAD
    zeros_c = jnp.zeros((c, LANES), F32)
    nsub = rows // c

    def head_lo(x, h):
        s = x[:, (h // 2) * LANES:(h // 2 + 1) * LANES]
        if h % 2:
            s = pltpu.roll(s, RWKV_HEAD, 1)
        return jnp.where(lo_half[0:x.shape[0]], s, 0.0)

    def head_hi(x, h):
        s = x[:, (h // 2) * LANES:(h // 2 + 1) * LANES]
        if h % 2 == 0:
            s = pltpu.roll(s, RWKV_HEAD, 1)
        return jnp.where(lo_half[0:x.shape[0]], 0.0, s)

    chains = [(s, h) for s in range(nsub) for h in range(RWKV_HEADS)]
    rh, vh, w_bot, pw, x, f, e_hat, g_hat = {}, {}, {}, {}, {}, {}, {}, {}
    lo_cols = ci < RWKV_HEAD
    for s in range(nsub):
        sl = slice(s * c, (s + 1) * c)
        lw = lw_sc[sl, :]
        cum = _mm_exact_lhs(tri, lw)
        tot = jnp.sum(lw, axis=0, keepdims=True)
        kn_c, a_c, kd_c = kn_sc[sl, :], a_sc[sl, :], kd_sc[sl, :]
        e_neg = jnp.exp(-cum)
        e_end = jnp.exp(tot - cum)
        r_t = r_sc[sl, :] * jnp.exp(cum)
        a_t = -kn_c * jnp.exp(cum - lw)
        b_t = kn_c * a_c * e_neg
        k_t = kd_c * e_neg
        b_e = kn_c * a_c * e_end
        k_e = kd_c * e_end
        g_end = jnp.exp(tot)
        v_c = v_sc[sl, :]
        for h in range(RWKV_HEADS):
            a_lo = head_lo(a_t, h)
            rh[s, h], vh[s, h] = head_lo(r_t, h), head_hi(v_c, h)
            g = _mm_nt(jnp.concatenate([a_lo, rh[s, h]], 0),
                       jnp.concatenate([head_lo(b_t, h), head_lo(k_t, h)], 0))
            g = jnp.where(keep, g, 0.0)
            w_bot[s, h] = g[c:2 * c]
            pw[s, h] = jnp.where(lo_half, g[0:c], 0.0)
            x[s, h] = (a_lo, jnp.where(lo_half, 0.0, g[0:c]))
            e_hat[s, h] = jnp.concatenate([head_lo(b_e, h), head_lo(k_e, h)], 0)
            g_hat[s, h] = head_lo(g_end, h)
    for ch in chains:
        a_lo, l_ak = x[ch]
        x[ch] = a_lo + _mm(l_ak, jnp.concatenate([zeros_c, vh[ch]], 0))
    zeros_xp = jnp.zeros((c, 2 * LANES), F32)
    for step in range(6):
        for ch in chains:
            if step < 5:
                z = _mm(pw[ch], jnp.concatenate([jnp.concatenate([x[ch], pw[ch]], 1), zeros_xp], 0))
                x[ch] = x[ch] + z[:, 0:LANES]
                pw[ch] = z[:, LANES:2 * LANES]
            else:
                x[ch] = x[ch] + _mm(pw[ch], jnp.concatenate([x[ch], zeros_c], 0))
    for ch in chains:
        f[ch] = jnp.concatenate([x[ch], vh[ch]], 0)
        wf = _mm(w_bot[ch], f[ch])
        lhs_sc[ch[0], ch[1], 2 * c:3 * c] = jnp.where(lo_half, rh[ch] + wf, 0.0).astype(BF16)
        yl_sc[ch[0], ch[1]] = jnp.where(lo_half, 0.0, wf)
    for ch in chains:
        mn = _mm_tn(e_hat[ch], f[ch])
        m_hi, m_lo = _split2((jnp.where(lo_cols, mn, 0.0) + jnp.where(eye, g_hat[ch], 0.0))[0:c])
        lhs_sc[ch[0], ch[1], 0:c] = m_hi
        lhs_sc[ch[0], ch[1], c:2 * c] = m_lo
        nt_sc[ch[0], ch[1]] = jnp.where(lo_cols, 0.0, mn)[0:c]

    def advance(s, carry):
        sub = s + d * (nsub - 1 - 2 * s)
        sl = pl.ds(pl.multiple_of(sub * c, c), c)
        for j in range(RWKV_HEADS // 2):
            ys = []
            for h in (2 * j, 2 * j + 1):
                h_hi, h_lo = _split2(h_sc[h])
                res = _mm(lhs_sc[sub, h], h_hi)
                h_new = res[0:c] + res[c:2 * c] + _mm(lhs_sc[sub, h, 0:c], h_lo) + nt_sc[sub, h]
                h_sc[h] = jnp.concatenate([h_new, zeros_c], 0)
                ys.append(res[2 * c:3 * c] + yl_sc[sub, h])
            y_ref[0, 0, sl, j * LANES:(j + 1) * LANES] = pltpu.roll(ys[0], RWKV_HEAD, 1) + ys[1]
        return carry

    lax.fori_loop(0, nsub, advance, 0)


def _rwkv_scan(p_rwkv, mu, w_lora, w0, a0, k_k, k_a, rows):
    bsz, t, _ = p_rwkv.shape
    nblk = t // rows
    w = RWKV_WIDTH
    blk_of = lambda d, b, i: i + d * (nblk - 1 - 2 * i)
    halo_prev, halo_next = _halo_specs(rows, t, blk_of)
    vec = lambda n: pl.BlockSpec((1, n), lambda d, b, i: (0, 0))
    dvec = lambda n: pl.BlockSpec((1, 1, n), lambda d, b, i: (d, 0, 0))
    blocks = (_nbytes((rows + 16, RWKV_IN), F32) + _nbytes((LANES, 2 * w), F32) + _nbytes((rows, w), F32))
    per_chain = (rows // CHUNK, RWKV_HEADS, CHUNK, LANES)
    lhs_rows = (rows // CHUNK, RWKV_HEADS, 3 * CHUNK, LANES)
    scratch = (6 * _nbytes((rows, w), F32) + _nbytes((RWKV_HEADS, LANES, LANES), F32)
               + 2 * _nbytes(per_chain, F32) + _nbytes(lhs_rows, BF16))
    return pl.pallas_call(
        functools.partial(_rwkv_scan_kernel, rows=rows),
        grid=(2, bsz, nblk),
        in_specs=[
            pl.BlockSpec((1, rows, RWKV_IN), lambda d, b, i: (b, blk_of(d, b, i), 0)),
            halo_prev, halo_next,
            vec(RWKV_IN),
            pl.BlockSpec((1, LANES, 2 * w), lambda d, b, i: (d, 0, 0)),
            dvec(w), dvec(w), vec(w), vec(w),
        ],
        out_specs=pl.BlockSpec((1, 1, rows, w), lambda d, b, i: (d, b, blk_of(d, b, i), 0)),
        out_shape=jax.ShapeDtypeStruct((2, bsz, t, w), F32),
        scratch_shapes=([pltpu.VMEM((rows, w), F32)] * 6 + [pltpu.VMEM((RWKV_HEADS, LANES, LANES), F32)]
                        + [pltpu.VMEM(lhs_rows, BF16)] + [pltpu.VMEM(per_chain, F32)] * 2),
        compiler_params=pltpu.CompilerParams(
            dimension_semantics=("parallel", "parallel", "arbitrary"),
            vmem_limit_bytes=_vmem_limit(blocks, scratch + 8 * _nbytes((rows, RWKV_IN), F32))),
        name="rwkv_scan",
    )(p_rwkv, p_rwkv, p_rwkv, mu, w_lora, w0, a0, k_k, k_a)


def _mix_kernel(x_ref, mod_ref, ot_ref, y_ref, pr_ref, hp_ref, hn_ref, pg_ref, mu_ref, wi_ref, a0_ref, ka_ref,
                rk_ref, wg_ref, lw_ref, lb_ref, wmo_ref, wro_ref, wo_ref, o_ref):
    i = pl.program_id(1)
    w = RWKV_WIDTH
    prev_row, next_row = _halo_rows(hp_ref, hn_ref, i, pl.num_programs(1))
    xs = _shift_mix(pr_ref[0], prev_row, next_row, mu_ref[...])
    r, k, v = xs[:, 0:w], xs[:, w:2 * w], xs[:, 2 * w:3 * w]
    pre = _mm3(xs[:, LORA_OFF:LORA_OFF + LANES], wi_ref[...])
    ka = ka_ref[...]
    bonus_k = (k * (1.0 + (jax.nn.sigmoid(a0_ref[0] + pre[:, 0:w]) - 1.0) * ka)
               + k * (1.0 + (jax.nn.sigmoid(a0_ref[1] + pre[:, w:2 * w]) - 1.0) * ka))
    gate = _mm(jax.nn.sigmoid(xs[:, GATE_OFF:GATE_OFF + GATE_LORA]), wg_ref[...])
    ones_h = _head_ones(w, RWKV_HEAD)
    inv_n = 1.0 / RWKV_HEAD
    y = y_ref[0, 0] + y_ref[1, 0]
    mean = _mm_exact_rhs(y, ones_h) * inv_n
    yc = y - mean
    var = _mm_exact_rhs(yc * yc, ones_h) * inv_n
    yn = yc * lax.rsqrt(var + LNX_EPS) * lw_ref[...] + lb_ref[...]
    bonus = _mm_exact_rhs(r * bonus_k * rk_ref[...], ones_h) * v
    o_rwkv = _mm((yn + bonus) * gate, wro_ref[...])
    o_mla = _mm_tn(ot_ref[0], wmo_ref[...])
    pg = pg_ref[0].astype(F32)
    merged = jax.nn.sigmoid(pg[:, 0:D_MODEL]) * o_mla + jax.nn.sigmoid(pg[:, D_MODEL:2 * D_MODEL]) * o_rwkv
    g1 = mod_ref[0, :, 2 * D_MODEL:3 * D_MODEL]
    o_ref[0] = x_ref[0] + g1 * _mm(merged, wo_ref[...])


def _mix_out(x, mod3, o_t, y, p_rwkv, p_gate, mu, w_iclr, a0, k_a, r_k, w_gate, lnx_w, lnx_b,
             w_mla_o, w_rwkv_o, w_out, tm):
    bsz, t, _ = x.shape
    w = RWKV_WIDTH
    halo_prev, halo_next = _halo_specs(tm, t, lambda b, i: i)
    const2 = lambda b, i: (0, 0)
    vec = lambda n: pl.BlockSpec((1, n), const2)
    blocks = (2 * _nbytes((tm, D_MODEL), F32) + _nbytes((MLA_WIDTH, tm), BF16) + 2 * _nbytes((tm, w), F32)
              + _nbytes((tm + 16, RWKV_IN), F32) + _nbytes((tm, GATE_IN), BF16)
              + _nbytes((LANES, 2 * w), F32) + _nbytes((GATE_LORA, w), BF16)
              + 2 * _nbytes((w, D_MODEL), BF16) + _nbytes((D_MODEL, D_MODEL), BF16))
    return pl.pallas_call(
        _mix_kernel,
        grid=(bsz, t // tm),
        in_specs=[
            pl.BlockSpec((1, tm, D_MODEL), lambda b, i: (b, i, 0)),
            pl.BlockSpec((1, 1, 6 * D_MODEL), lambda b, i: (b, 0, 0)),
            pl.BlockSpec((1, MLA_WIDTH, tm), lambda b, i: (b, 0, i)),
            pl.BlockSpec((2, 1, tm, w), lambda b, i: (0, b, i, 0)),
            pl.BlockSpec((1, tm, RWKV_IN), lambda b, i: (b, i, 0)),
            halo_prev, halo_next,
            pl.BlockSpec((1, tm, GATE_IN), lambda b, i: (b, i, 0)),
            vec(RWKV_IN),
            pl.BlockSpec((LANES, 2 * w), const2),
            pl.BlockSpec((2, 1, w), lambda b, i: (0, 0, 0)),
            vec(w), vec(w),
            pl.BlockSpec((GATE_LORA, w), const2),
            vec(w), vec(w),
            pl.BlockSpec((MLA_WIDTH, D_MODEL), const2),
            pl.BlockSpec((w, D_MODEL), const2),
            pl.BlockSpec((D_MODEL, D_MODEL), const2),
        ],
        out_specs=pl.BlockSpec((1, tm, D_MODEL), lambda b, i: (b, i, 0)),
        out_shape=jax.ShapeDtypeStruct((bsz, t, D_MODEL), F32),
        compiler_params=pltpu.CompilerParams(
            dimension_semantics=("parallel", "parallel"),
            vmem_limit_bytes=_vmem_limit(blocks, 10 * _nbytes((tm, RWKV_IN), F32))),
        name="mix_out",
    )(x, mod3, o_t, y, p_rwkv, p_rwkv, p_rwkv, p_gate, mu, w_iclr, a0, k_a, r_k, w_gate, lnx_w, lnx_b,
      w_mla_o, w_rwkv_o, w_out)


def _ffn_kernel(x_ref, mod_ref, nw_ref, wi_ref, wo_ref, fn_ref, o_ref, acc_ref, *, tf):
    x = x_ref[0]
    sh = mod_ref[0, :, 3 * D_MODEL:4 * D_MODEL]
    sc = mod_ref[0, :, 4 * D_MODEL:5 * D_MODEL]
    g2 = mod_ref[0, :, 5 * D_MODEL:6 * D_MODEL]
    h = (_rmsnorm(x, nw_ref[...]) * (1.0 + sc) + sh).astype(BF16)
    for j in range(D_FF // tf):
        u = _mm(h, wi_ref[:, j * tf:(j + 1) * tf])
        zg = _mm(h, wi_ref[:, D_FF + j * tf:D_FF + (j + 1) * tf])
        part = _mm(u * jax.nn.sigmoid(u) * zg, wo_ref[j * tf:(j + 1) * tf, :])
        if j == 0:
            acc_ref[...] = part
        else:
            acc_ref[...] += part
    o_ref[0] = _rmsnorm(x + g2 * acc_ref[...], fn_ref[...])


def _ffn(x, mod3, norm_w, w_in, w_out, final_w, tm, tf):
    bsz, t, _ = x.shape
    assert D_FF % tf == 0 and tf % LANES == 0
    blocks = (2 * _nbytes((tm, D_MODEL), F32) + _nbytes((D_MODEL, 2 * D_FF), BF16)
              + _nbytes((D_FF, D_MODEL), BF16))
    const2 = lambda b, i: (0, 0)
    return pl.pallas_call(
        functools.partial(_ffn_kernel, tf=tf),
        grid=(bsz, t // tm),
        in_specs=[
            pl.BlockSpec((1, tm, D_MODEL), lambda b, i: (b, i, 0)),
            pl.BlockSpec((1, 1, 6 * D_MODEL), lambda b, i: (b, 0, 0)),
            pl.BlockSpec((1, D_MODEL), const2),
            pl.BlockSpec((D_MODEL, 2 * D_FF), const2),
            pl.BlockSpec((D_FF, D_MODEL), const2),
            pl.BlockSpec((1, D_MODEL), const2),
        ],
        out_specs=pl.BlockSpec((1, tm, D_MODEL), lambda b, i: (b, i, 0)),
        out_shape=jax.ShapeDtypeStruct((bsz, t, D_MODEL), F32),
        scratch_shapes=[pltpu.VMEM((tm, D_MODEL), F32)],
        compiler_params=pltpu.CompilerParams(
            dimension_semantics=("parallel", "parallel"),
            vmem_limit_bytes=_vmem_limit(blocks, _nbytes((tm, D_MODEL), F32) + 6 * _nbytes((tm, tf), F32))),
        name="ffn",
    )(x, mod3, norm_w, w_in, w_out, final_w)


def _rope_tables(t):
    half = QK_ROPE // 2
    inv = ROPE_BASE ** (-jnp.arange(half, dtype=F32) / half)
    ang = jnp.arange(t, dtype=F32)[:, None] * inv[None, :]
    cos, sin = jnp.cos(ang), jnp.sin(ang)
    z = lambda n: jnp.zeros((t, n), F32)
    rc = jnp.concatenate([jnp.ones((t, QK_NOPE), F32), cos, cos, z(LANES - QK_NOPE - QK_ROPE)], 1)
    rs1 = jnp.concatenate([z(QK_NOPE + half), sin, z(LANES - QK_NOPE - QK_ROPE)], 1)
    rs2 = jnp.concatenate([z(QK_NOPE), -sin, z(LANES - QK_NOPE - half)], 1)
    return rc, rs1, rs2


def _prepare_params(w_in, w_uq, w_ukv, w_decay_up, w_iclr_up, w_gate_up, w_mla_o, w_rwkv_o, w_out,
                    w_ffn_in, w_ffn_out):
    dqk = QK_NOPE + QK_ROPE
    kpe_tile = jnp.zeros((D_MODEL, LANES), F32).at[:, QK_NOPE:QK_NOPE + QK_ROPE].set(
        w_in[:, Q_LORA + KV_LORA:MLA_IN])
    w_in_p = jnp.concatenate([w_in[:, 0:Q_LORA + KV_LORA], kpe_tile, w_in[:, MLA_IN:]], 1).astype(BF16)
    wq = w_uq.reshape(Q_LORA, MLA_HEADS, dqk)
    wq_p = jnp.pad(wq, ((0, 0), (0, 0), (0, HEAD_PAD - dqk))).reshape(Q_LORA, MLA_HEADS * HEAD_PAD).astype(BF16)
    wkv = w_ukv.reshape(KV_LORA, MLA_HEADS, QK_NOPE + V_HEAD)
    wk_p = jnp.pad(wkv[:, :, 0:QK_NOPE], ((0, 0), (0, 0), (0, HEAD_PAD - QK_NOPE))).reshape(
        KV_LORA, MLA_HEADS * HEAD_PAD).astype(BF16)
    wvt = wkv[:, :, QK_NOPE:].reshape(KV_LORA, MLA_WIDTH).T.astype(BF16)
    zl = jnp.zeros((DECAY_LORA, RWKV_WIDTH), F32)
    w_lora = jnp.stack([jnp.concatenate([jnp.concatenate([w_decay_up[d], zl], 1),
                                         jnp.concatenate([zl, w_iclr_up[d]], 1)], 0) for d in range(2)])
    w_iclr = jnp.concatenate([jnp.zeros((DECAY_LORA, 2 * RWKV_WIDTH), F32),
                              jnp.concatenate([w_iclr_up[0], w_iclr_up[1]], 1)], 0)
    return dict(w_in_p=w_in_p, wq_p=wq_p, wk_p=wk_p, wvt=wvt, w_lora=w_lora, w_iclr=w_iclr,
                w_gate=w_gate_up.astype(BF16), w_mla_o=w_mla_o.astype(BF16), w_rwkv_o=w_rwkv_o.astype(BF16),
                w_out=w_out.astype(BF16), w_ffn_in=w_ffn_in.astype(BF16), w_ffn_out=w_ffn_out.astype(BF16))


def _tiles(t):
    return dict(tm_proj=min(512, t), tm_prep=min(1024, t), tq=min(2048, t), tk=min(1024, t),
                rows_scan=min(512, t), tm_mix=min(512, t), tm_ffn=min(512, t), tf=256)


def _encoder(x, mod, pp, norm_mix, q_a_norm, kv_a_norm, mu_shift, w0, a0, k_k, k_a, r_k, lnx_w, lnx_b,
             norm_ffn, final_norm):
    bsz, t, _ = x.shape
    ts = _tiles(t)
    row = lambda v: v.reshape(1, -1)
    mod3 = mod.reshape(bsz, 1, 6 * D_MODEL)
    p_mla, p_rwkv, p_gate = _in_proj(x, mod3, row(norm_mix), pp["w_in_p"], ts["tm_proj"])
    q, k, vt = _mla_prep(p_mla, row(q_a_norm), row(kv_a_norm), pp["wq_p"], pp["wk_p"], pp["wvt"],
                         *_rope_tables(t), ts["tm_prep"], ts["tk"])
    o_t = _attention(q, k, vt, ts["tq"])
    y = _rwkv_scan(p_rwkv, row(mu_shift), pp["w_lora"], w0.reshape(2, 1, -1), a0.reshape(2, 1, -1),
                   row(k_k), row(k_a), ts["rows_scan"])
    x1 = _mix_out(x, mod3, o_t, y, p_rwkv, p_gate, row(mu_shift), pp["w_iclr"], a0.reshape(2, 1, -1), row(k_a),
                  row(r_k), pp["w_gate"], row(lnx_w), row(lnx_b), pp["w_mla_o"], pp["w_rwkv_o"], pp["w_out"],
                  ts["tm_mix"])
    return _ffn(x1, mod3, row(norm_ffn), pp["w_ffn_in"], pp["w_ffn_out"], row(final_norm), ts["tm_ffn"], ts["tf"])


def kernel(x_prompt, x_sample, c_prompt, c_sample, w_ada, b_ada, norm_mix, w_in, q_a_norm, kv_a_norm, w_uq, w_ukv, mu_shift, w0, w_decay_up, a0, w_iclr_up, w_gate_up, k_k, k_a, r_k, lnx_w, lnx_b, w_mla_o, w_rwkv_o, w_out, norm_ffn, w_ffn_in, w_ffn_out, final_norm):
    pp = _prepare_params(w_in[0], w_uq[0], w_ukv[0], w_decay_up[0], w_iclr_up[0], w_gate_up[0], w_mla_o[0],
                         w_rwkv_o[0], w_out[0], w_ffn_in[0], w_ffn_out[0])
    nb = x_prompt.shape[0]
    mod = _adaln_mod(jnp.concatenate([c_prompt, c_sample], 0), w_ada[0], b_ada[0])
    args = (pp, norm_mix[0], q_a_norm[0], kv_a_norm[0], mu_shift[0], w0[0], a0[0], k_k[0], k_a[0], r_k[0],
            lnx_w[0], lnx_b[0], norm_ffn[0], final_norm)
    return (_encoder(x_prompt, mod[:nb], *args), _encoder(x_sample, mod[nb:], *args))
```

```python
import functools

import jax
import jax.numpy as jnp
from jax import lax
from jax.experimental import pallas as pl
from jax.experimental.pallas import tpu as pltpu

F32 = jnp.float32
BF16 = jnp.bfloat16

D_MODEL = 1024
MLA_HEADS = 8
QK_NOPE = 64
QK_ROPE = 32
V_HEAD = 64
V_ROWS = 80
Q_LORA = 384
KV_LORA = 256
MLA_WIDTH = MLA_HEADS * V_HEAD
ROPE_BASE = 10000.0
RWKV_HEADS = 8
RWKV_HEAD = 64
RWKV_WIDTH = RWKV_HEADS * RWKV_HEAD
DECAY_LORA = 64
ICLR_LORA = 64
GATE_LORA = 128
D_FF = 2816
EPS = 1e-6
LNX_EPS = 64e-5
MLA_IN = Q_LORA + KV_LORA + QK_ROPE
RWKV_IN = 3 * RWKV_WIDTH + DECAY_LORA + ICLR_LORA + GATE_LORA
GATE_IN = 2 * D_MODEL

LANES = 128
SUBLANES = 8
HEAD_PAD = LANES
MLA_IN_PAD = Q_LORA + KV_LORA + LANES
IN_COLS_PAD = MLA_IN_PAD + RWKV_IN + GATE_IN
LORA_OFF = 3 * RWKV_WIDTH
GATE_OFF = LORA_OFF + DECAY_LORA + ICLR_LORA
VMEM_PHYS_BYTES = 64 * 1024 * 1024
VMEM_CAP_BYTES = 60000 * 1024
LOG2E = 1.4426950408889634
CHUNK = 64


def _vmem_limit(block_bytes, temp_bytes):
    return int(min(2 * block_bytes + temp_bytes, VMEM_CAP_BYTES))


def _nbytes(shape, dtype):
    n = 1
    for s in shape:
        n *= s
    return n * jnp.dtype(dtype).itemsize


def _bf(x):
    return x if x.dtype == BF16 else x.astype(BF16)


def _mm(a, b):
    return jnp.dot(_bf(a), _bf(b), preferred_element_type=F32)


def _mm_nt(a, b):
    return lax.dot_general(_bf(a), _bf(b), (((1,), (1,)), ((), ())), preferred_element_type=F32)


def _mm_tn(a, b):
    return lax.dot_general(_bf(a), _bf(b), (((0,), (0,)), ((), ())), preferred_element_type=F32)


def _split2(x):
    hi = x.astype(BF16)
    lo = (x - hi.astype(F32)).astype(BF16)
    return hi, lo


def _mm3(a, b):
    ah, al = _split2(a)
    bh, bl = _split2(b)
    return _mm(ah, bh) + (_mm(ah, bl) + _mm(al, bh))


def _mm_exact_lhs(a_bf, b):
    b0, b1 = _split2(b)
    return _mm(a_bf, b0) + _mm(a_bf, b1)


def _mm_exact_rhs(a, b_bf):
    a0, a1 = _split2(a)
    return _mm(a0, b_bf) + _mm(a1, b_bf)


def _head_ones(n, head):
    ri = lax.broadcasted_iota(jnp.int32, (n, n), 0) // head
    ci = lax.broadcasted_iota(jnp.int32, (n, n), 1) // head
    return jnp.where(ri == ci, 1.0, 0.0).astype(BF16)


def _rmsnorm(x, g):
    return x * lax.rsqrt(jnp.mean(x * x, axis=-1, keepdims=True) + EPS) * g


def _softplus(x):
    return jnp.maximum(x, 0.0) + jnp.log(1.0 + jnp.exp(-jnp.abs(x)))


def _mod_kernel(c_ref, w_ref, b_ref, o_ref):
    c = c_ref[...]
    o_ref[...] = _mm3(c * jax.nn.sigmoid(c), w_ref[...]) + b_ref[...]


def _adaln_mod(c_all, w_ada, b_ada):
    rows, n = c_all.shape[0], w_ada.shape[1]
    tn = 1536
    blocks = _nbytes((rows, D_MODEL), F32) + _nbytes((D_MODEL, tn), F32) + _nbytes((rows + 1, tn), F32)
    return pl.pallas_call(
        _mod_kernel,
        grid=(n // tn,),
        in_specs=[
            pl.BlockSpec((rows, D_MODEL), lambda j: (0, 0)),
            pl.BlockSpec((D_MODEL, tn), lambda j: (0, j)),
            pl.BlockSpec((1, tn), lambda j: (0, j)),
        ],
        out_specs=pl.BlockSpec((rows, tn), lambda j: (0, j)),
        out_shape=jax.ShapeDtypeStruct((rows, n), F32),
        compiler_params=pltpu.CompilerParams(
            dimension_semantics=("arbitrary",),
            vmem_limit_bytes=_vmem_limit(blocks, 3 * _nbytes((D_MODEL, tn), F32))),
        name="adaln_mod",
    )(c_all, w_ada, b_ada.reshape(1, n))


def _inproj_kernel(x_ref, xp_ref, xn_ref, mod_ref, nw_ref, mu_ref, w_ref, pm_ref, xs_ref, pg_ref):
    i = pl.program_id(1)
    tm = x_ref.shape[1]
    sh = mod_ref[0, :, 0:D_MODEL]
    sc = mod_ref[0, :, D_MODEL:2 * D_MODEL]
    x_ext = jnp.concatenate([xp_ref[0], x_ref[0], xn_ref[0]], 0)
    h_f32 = _rmsnorm(x_ext, nw_ref[...]) * (1.0 + sc) + sh
    h_ext = h_f32.astype(BF16)
    h = h_f32[SUBLANES:SUBLANES + tm].astype(BF16)
    a, b = MLA_IN_PAD, MLA_IN_PAD + RWKV_IN
    pm_ref[0] = _mm(h, w_ref[:, 0:a]).astype(BF16)
    pg_ref[0] = _mm(h, w_ref[:, b:IN_COLS_PAD]).astype(BF16)
    p = _mm(h_ext, w_ref[:, a:b])
    rid = lax.broadcasted_iota(jnp.int32, (tm + 2 * SUBLANES, 1), 0)
    outside = ((rid < SUBLANES) & (i == 0)) | ((rid >= tm + SUBLANES) & (i == pl.num_programs(1) - 1))
    p = jnp.where(outside, 0.0, p)
    n = tm + 2 * SUBLANES
    shifted = 0.5 * (pltpu.roll(p, 1, 0) + pltpu.roll(p, n - 1, 0))
    xs = p + mu_ref[...] * (shifted - p)
    xs_ref[0] = xs[SUBLANES:SUBLANES + tm]


def _in_proj(x, mod3, norm_w, mu, w_in_p, tm):
    bsz, t, _ = x.shape
    per = tm // SUBLANES
    last = t // SUBLANES - 1
    blocks = (_nbytes((tm + 2 * SUBLANES, D_MODEL), F32) + _nbytes((D_MODEL, IN_COLS_PAD), BF16)
              + _nbytes((tm, MLA_IN_PAD), BF16) + _nbytes((tm, RWKV_IN), F32) + _nbytes((tm, GATE_IN), BF16))
    return pl.pallas_call(
        _inproj_kernel,
        grid=(bsz, t // tm),
        in_specs=[
            pl.BlockSpec((1, tm, D_MODEL), lambda b, i: (b, i, 0)),
            pl.BlockSpec((1, SUBLANES, D_MODEL), lambda b, i: (b, jnp.maximum(i * per - 1, 0), 0)),
            pl.BlockSpec((1, SUBLANES, D_MODEL), lambda b, i: (b, jnp.minimum((i + 1) * per, last), 0)),
            pl.BlockSpec((1, 1, 6 * D_MODEL), lambda b, i: (b, 0, 0)),
            pl.BlockSpec((1, D_MODEL), lambda b, i: (0, 0)),
            pl.BlockSpec((1, RWKV_IN), lambda b, i: (0, 0)),
            pl.BlockSpec((D_MODEL, IN_COLS_PAD), lambda b, i: (0, 0)),
        ],
        out_specs=[
            pl.BlockSpec((1, tm, MLA_IN_PAD), lambda b, i: (b, i, 0)),
            pl.BlockSpec((1, tm, RWKV_IN), lambda b, i: (b, i, 0)),
            pl.BlockSpec((1, tm, GATE_IN), lambda b, i: (b, i, 0)),
        ],
        out_shape=[
            jax.ShapeDtypeStruct((bsz, t, MLA_IN_PAD), BF16),
            jax.ShapeDtypeStruct((bsz, t, RWKV_IN), F32),
            jax.ShapeDtypeStruct((bsz, t, GATE_IN), BF16),
        ],
        compiler_params=pltpu.CompilerParams(
            dimension_semantics=("parallel", "parallel"),
            vmem_limit_bytes=_vmem_limit(blocks, 2 * _nbytes((tm, IN_COLS_PAD), F32))),
        name="in_proj",
    )(x, x, x, mod3, norm_w, mu, w_in_p)


def _mla_prep_kernel(pm_ref, qn_ref, kvn_ref, wq_ref, wk_ref, wvt_ref, rc_ref, rs1_ref, rs2_ref,
                     q_ref, k_ref, vt_ref, *, tk):
    p = pm_ref[0].astype(F32)
    cq = _rmsnorm(p[:, 0:Q_LORA], qn_ref[...]).astype(BF16)
    ckv = _rmsnorm(p[:, Q_LORA:Q_LORA + KV_LORA], kvn_ref[...]).astype(BF16)
    rc, rs1, rs2 = rc_ref[...], rs1_ref[...], rs2_ref[...]

    def rope(x):
        return x * rc + pltpu.roll(x, QK_ROPE // 2, 1) * rs1 + pltpu.roll(x, LANES - QK_ROPE // 2, 1) * rs2

    k_pe = rope(p[:, Q_LORA + KV_LORA:MLA_IN_PAD])
    q = _mm(cq, wq_ref[...])
    kn = _mm(ckv, wk_ref[...])
    vt = _mm_nt(wvt_ref[...], ckv)
    scale = (QK_NOPE + QK_ROPE) ** -0.5 * LOG2E
    tail = jnp.where(lax.broadcasted_iota(jnp.int32, (V_ROWS - V_HEAD, tk), 0) == 0, 1.0, 0.0)
    for h in range(MLA_HEADS):
        sl = slice(h * HEAD_PAD, (h + 1) * HEAD_PAD)
        q_ref[0, h] = (rope(q[:, sl]) * scale).astype(BF16)
        k_ref[0, h] = (kn[:, sl] + k_pe).astype(BF16)
        for c in range(vt.shape[1] // tk):
            vt_ref[0, h, c] = jnp.concatenate(
                [vt[h * V_HEAD:(h + 1) * V_HEAD, c * tk:(c + 1) * tk], tail], 0).astype(BF16)


def _mla_prep(p_mla, q_norm, kv_norm, wq_p, wk_p, wvt, rope_c, rope_s1, rope_s2, tm, tk):
    bsz, t, _ = p_mla.shape
    assert tm % tk == 0
    hp = MLA_HEADS * HEAD_PAD
    blocks = (_nbytes((tm, MLA_IN_PAD), BF16) + _nbytes((Q_LORA + KV_LORA, hp), BF16)
              + _nbytes((MLA_WIDTH, KV_LORA), BF16) + 3 * _nbytes((tm, LANES), F32)
              + 2 * _nbytes((tm, hp), BF16) + _nbytes((MLA_WIDTH, tm), BF16))
    const = lambda b, i: (0, 0)
    return pl.pallas_call(
        functools.partial(_mla_prep_kernel, tk=tk),
        grid=(bsz, t // tm),
        in_specs=[
            pl.BlockSpec((1, tm, MLA_IN_PAD), lambda b, i: (b, i, 0)),
            pl.BlockSpec((1, Q_LORA), const),
            pl.BlockSpec((1, KV_LORA), const),
            pl.BlockSpec((Q_LORA, hp), const),
            pl.BlockSpec((KV_LORA, hp), const),
            pl.BlockSpec((MLA_WIDTH, KV_LORA), const),
            pl.BlockSpec((tm, LANES), lambda b, i: (i, 0)),
            pl.BlockSpec((tm, LANES), lambda b, i: (i, 0)),
            pl.BlockSpec((tm, LANES), lambda b, i: (i, 0)),
        ],
        out_specs=[
            pl.BlockSpec((1, MLA_HEADS, tm, HEAD_PAD), lambda b, i: (b, 0, i, 0)),
            pl.BlockSpec((1, MLA_HEADS, tm, HEAD_PAD), lambda b, i: (b, 0, i, 0)),
            pl.BlockSpec((1, MLA_HEADS, tm // tk, V_ROWS, tk), lambda b, i: (b, 0, i, 0, 0)),
        ],
        out_shape=[
            jax.ShapeDtypeStruct((bsz, MLA_HEADS, t, HEAD_PAD), BF16),
            jax.ShapeDtypeStruct((bsz, MLA_HEADS, t, HEAD_PAD), BF16),
            jax.ShapeDtypeStruct((bsz, MLA_HEADS, t // tk, V_ROWS, tk), BF16),
        ],
        compiler_params=pltpu.CompilerParams(
            dimension_semantics=("parallel", "parallel"),
            vmem_limit_bytes=_vmem_limit(blocks, 6 * _nbytes((tm, hp), F32))),
        name="mla_prep",
    )(p_mla, q_norm, kv_norm, wq_p, wk_p, wvt, rope_c, rope_s1, rope_s2)


def _attn_kernel(q_ref, k_ref, vt_ref, o_ref):
    q = q_ref[0, 0]
    tq = q.shape[0]
    nk, tk = vt_ref.shape[2], vt_ref.shape[4]

    def body(j, carry):
        m, acc = carry
        r0 = pl.multiple_of(j * tk, tk)
        st = _mm_nt(k_ref[0, 0, pl.ds(r0, tk), :], q)
        m_new = jnp.maximum(m, jnp.max(st, axis=0, keepdims=True))
        p = jnp.exp2(st - m_new)
        acc = jnp.exp2(m - m_new) * acc + _mm(vt_ref[0, 0, j], p)
        return m_new, acc

    init = (jnp.full((1, tq), -jnp.inf, F32), jnp.zeros((V_ROWS, tq), F32))
    _, acc = lax.fori_loop(0, nk, body, init, unroll=min(nk, 4))
    o_ref[0] = (acc[0:V_HEAD] / acc[V_HEAD:V_HEAD + 1]).astype(BF16)


def _attention(q, k, vt, tq):
    bsz, nh, t, _ = q.shape
    nk, tk = vt.shape[2], vt.shape[4]
    blocks = (_nbytes((tq, HEAD_PAD), BF16) + _nbytes((t, HEAD_PAD), BF16) + _nbytes((V_ROWS, t), BF16)
              + _nbytes((V_HEAD, tq), BF16))
    return pl.pallas_call(
        _attn_kernel,
        grid=(bsz, nh, t // tq),
        in_specs=[
            pl.BlockSpec((1, 1, tq, HEAD_PAD), lambda b, h, i: (b, h, i, 0)),
            pl.BlockSpec((1, 1, t, HEAD_PAD), lambda b, h, i: (b, h, 0, 0)),
            pl.BlockSpec((1, 1, nk, V_ROWS, tk), lambda b, h, i: (b, h, 0, 0, 0)),
        ],
        out_specs=pl.BlockSpec((1, V_HEAD, tq), lambda b, h, i: (b, h, i)),
        out_shape=jax.ShapeDtypeStruct((bsz, nh * V_HEAD, t), BF16),
        compiler_params=pltpu.CompilerParams(
            dimension_semantics=("parallel", "parallel", "arbitrary"),
            vmem_limit_bytes=_vmem_limit(blocks, 6 * _nbytes((tk, tq), F32))),
        name="mla_attn",
    )(q, k, vt)


def _rwkv_scan_kernel(xs_ref, wl_ref, w0_ref, a0_ref, kk_ref, ka_ref, y_ref,
                      r_sc, v_sc, kn_sc, lw_sc, a_sc, kd_sc, h_sc, lhs_sc, yl_sc, nt_sc, *, rows):
    d = pl.program_id(0)
    i = pl.program_id(2)
    c = CHUNK
    w = RWKV_WIDTH

    @pl.when(i == 0)
    def _():
        h_sc[...] = jnp.zeros_like(h_sc)

    xs = xs_ref[0]
    k = xs[:, w:2 * w]
    z = xs[:, LORA_OFF:LORA_OFF + LANES]
    lane = lax.broadcasted_iota(jnp.int32, z.shape, 1)
    pre = _mm3(jnp.where(lane < DECAY_LORA, jnp.tanh(z), z), wl_ref[0])
    wlog = -_softplus(-(w0_ref[0] + pre[:, 0:w])) - 0.5
    a = jax.nn.sigmoid(a0_ref[0] + pre[:, w:2 * w])
    kn = k * kk_ref[...]
    ones_h = _head_ones(w, RWKV_HEAD)
    kn = kn * lax.rsqrt(_mm_exact_rhs(kn * kn, ones_h) + 1e-12)
    r_sc[...] = xs[:, 0:w]
    v_sc[...] = xs[:, 2 * w:3 * w]
    kn_sc[...] = kn
    lw_sc[...] = -jnp.exp(wlog)
    a_sc[...] = a
    kd_sc[...] = k * (1.0 + (a - 1.0) * ka_ref[...])

    ri = lax.broadcasted_iota(jnp.int32, (2 * c, 2 * c), 0)
    ci = lax.broadcasted_iota(jnp.int32, (2 * c, 2 * c), 1)
    rt, cs = ri % c, ci % c
    before = (rt - cs) * (1 - 2 * d) > 0
    keep = before | ((ri >= c) & (cs == rt))
    rt_c = lax.broadcasted_iota(jnp.int32, (c, c), 0)
    cs_c = lax.broadcasted_iota(jnp.int32, (c, c), 1)
    tri = jnp.where((rt_c - cs_c) * (1 - 2 * d) >= 0, 1.0, 0.0).astype(BF16)
    eye = ri == ci
    lane_c = lax.broadcasted_iota(jnp.int32, (c, LANES), 1)
    lo_half = lane_c < RWKV_HEAD
    zeros_c = jnp.zeros((c, LANES), F32)
    nsub = rows // c

    def head_lo(x, h):
        s = x[:, (h // 2) * LANES:(h // 2 + 1) * LANES]
        if h % 2:
            s = pltpu.roll(s, RWKV_HEAD, 1)
        return jnp.where(lo_half[0:x.shape[0]], s, 0.0)

    def head_hi(x, h):
        s = x[:, (h // 2) * LANES:(h // 2 + 1) * LANES]
        if h % 2 == 0:
            s = pltpu.roll(s, RWKV_HEAD, 1)
        return jnp.where(lo_half[0:x.shape[0]], 0.0, s)

    chains = [(s, h) for s in range(nsub) for h in range(RWKV_HEADS)]
    rh, vh, w_bot, pw, x, f, e_hat, g_hat = {}, {}, {}, {}, {}, {}, {}, {}
    lo_cols = ci < RWKV_HEAD
    for s in range(nsub):
        sl = slice(s * c, (s + 1) * c)
        lw = lw_sc[sl, :]
        cum = _mm_exact_lhs(tri, lw)
        tot = jnp.sum(lw, axis=0, keepdims=True)
        kn_c, a_c, kd_c = kn_sc[sl, :], a_sc[sl, :], kd_sc[sl, :]
        e_neg = jnp.exp(-cum)
        e_end = jnp.exp(tot - cum)
        r_t = r_sc[sl, :] * jnp.exp(cum)
        a_t = -kn_c * jnp.exp(cum - lw)
        b_t = kn_c * a_c * e_neg
        k_t = kd_c * e_neg
        b_e = kn_c * a_c * e_end
        k_e = kd_c * e_end
        g_end = jnp.exp(tot)
        v_c = v_sc[sl, :]
        for h in range(RWKV_HEADS):
            a_lo = head_lo(a_t, h)
            rh[s, h], vh[s, h] = head_lo(r_t, h), head_hi(v_c, h)
            g = _mm_nt(jnp.concatenate([a_lo, rh[s, h]], 0),
                       jnp.concatenate([head_lo(b_t, h), head_lo(k_t, h)], 0))
            g = jnp.where(keep, g, 0.0)
            w_bot[s, h] = g[c:2 * c]
            pw[s, h] = jnp.where(lo_half, g[0:c], 0.0)
            x[s, h] = (a_lo, jnp.where(lo_half, 0.0, g[0:c]))
            e_hat[s, h] = jnp.concatenate([head_lo(b_e, h), head_lo(k_e, h)], 0)
            g_hat[s, h] = head_lo(g_end, h)
    for ch in chains:
        a_lo, l_ak = x[ch]
        x[ch] = a_lo + _mm(l_ak, jnp.concatenate([zeros_c, vh[ch]], 0))
    zeros_xp = jnp.zeros((c, 2 * LANES), F32)
    for step in range(6):
        for ch in chains:
            if step < 5:
                z = _mm(pw[ch], jnp.concatenate([jnp.concatenate([x[ch], pw[ch]], 1), zeros_xp], 0))
                x[ch] = x[ch] + z[:, 0:LANES]
                pw[ch] = z[:, LANES:2 * LANES]
            else:
                x[ch] = x[ch] + _mm(pw[ch], jnp.concatenate([x[ch], zeros_c], 0))
    for ch in chains:
        f[ch] = jnp.concatenate([x[ch], vh[ch]], 0)
        wf = _mm(w_bot[ch], f[ch])
        lhs_sc[ch[0], ch[1], 2 * c:3 * c] = jnp.where(lo_half, rh[ch] + wf, 0.0).astype(BF16)
        yl_sc[ch[0], ch[1]] = jnp.where(lo_half, 0.0, wf)
    for ch in chains:
        mn = _mm_tn(e_hat[ch], f[ch])
        m_hi, m_lo = _split2((jnp.where(lo_cols, mn, 0.0) + jnp.where(eye, g_hat[ch], 0.0))[0:c])
        lhs_sc[ch[0], ch[1], 0:c] = m_hi
        lhs_sc[ch[0], ch[1], c:2 * c] = m_lo
        nt_sc[ch[0], ch[1]] = jnp.where(lo_cols, 0.0, mn)[0:c]

    def advance(s, carry):
        sub = s + d * (nsub - 1 - 2 * s)
        sl = pl.ds(pl.multiple_of(sub * c, c), c)
        for j in range(RWKV_HEADS // 2):
            ys = []
            for h in (2 * j, 2 * j + 1):
                h_hi, h_lo = _split2(h_sc[h])
                res = _mm(lhs_sc[sub, h], h_hi)
                h_new = res[0:c] + res[c:2 * c] + _mm(lhs_sc[sub, h, 0:c], h_lo) + nt_sc[sub, h]
                h_sc[h] = jnp.concatenate([h_new, zeros_c], 0)
                ys.append(res[2 * c:3 * c] + yl_sc[sub, h])
            y_ref[0, 0, sl, j * LANES:(j + 1) * LANES] = pltpu.roll(ys[0], RWKV_HEAD, 1) + ys[1]
        return carry

    lax.fori_loop(0, nsub, advance, 0)


def _rwkv_scan(xs, w_lora, w0, a0, k_k, k_a, rows):
    bsz, t, _ = xs.shape
    nblk = t // rows
    w = RWKV_WIDTH
    blk_of = lambda d, b, i: i + d * (nblk - 1 - 2 * i)
    vec = lambda n: pl.BlockSpec((1, n), lambda d, b, i: (0, 0))
    dvec = lambda n: pl.BlockSpec((1, 1, n), lambda d, b, i: (d, 0, 0))
    blocks = (_nbytes((rows, RWKV_IN), F32) + _nbytes((LANES, 2 * w), F32) + _nbytes((rows, w), F32))
    per_chain = (rows // CHUNK, RWKV_HEADS, CHUNK, LANES)
    lhs_rows = (rows // CHUNK, RWKV_HEADS, 3 * CHUNK, LANES)
    scratch = (6 * _nbytes((rows, w), F32) + _nbytes((RWKV_HEADS, LANES, LANES), F32)
               + 2 * _nbytes(per_chain, F32) + _nbytes(lhs_rows, BF16))
    return pl.pallas_call(
        functools.partial(_rwkv_scan_kernel, rows=rows),
        grid=(2, bsz, nblk),
        in_specs=[
            pl.BlockSpec((1, rows, RWKV_IN), lambda d, b, i: (b, blk_of(d, b, i), 0)),
            pl.BlockSpec((1, LANES, 2 * w), lambda d, b, i: (d, 0, 0)),
            dvec(w), dvec(w), vec(w), vec(w),
        ],
        out_specs=pl.BlockSpec((1, 1, rows, w), lambda d, b, i: (d, b, blk_of(d, b, i), 0)),
        out_shape=jax.ShapeDtypeStruct((2, bsz, t, w), F32),
        scratch_shapes=([pltpu.VMEM((rows, w), F32)] * 6 + [pltpu.VMEM((RWKV_HEADS, LANES, LANES), F32)]
                        + [pltpu.VMEM(lhs_rows, BF16)] + [pltpu.VMEM(per_chain, F32)] * 2),
        compiler_params=pltpu.CompilerParams(
            dimension_semantics=("parallel", "parallel", "arbitrary"),
            vmem_limit_bytes=_vmem_limit(blocks, scratch + 8 * _nbytes((rows, RWKV_IN), F32))),
        name="rwkv_scan",
    )(xs, w_lora, w0, a0, k_k, k_a)


def _mix_kernel(x_ref, mod_ref, ot_ref, y_ref, xs_ref, pg_ref, wi_ref, a0_ref, ka_ref,
                rk_ref, wg_ref, lw_ref, lb_ref, wmo_ref, wro_ref, wo_ref, o_ref):
    w = RWKV_WIDTH
    xs = xs_ref[0]
    r, k, v = xs[:, 0:w], xs[:, w:2 * w], xs[:, 2 * w:3 * w]
    pre = _mm(xs[:, LORA_OFF:LORA_OFF + LANES], wi_ref[...])
    ka = ka_ref[...]
    bonus_k = (k * (1.0 + (jax.nn.sigmoid(a0_ref[0] + pre[:, 0:w]) - 1.0) * ka)
               + k * (1.0 + (jax.nn.sigmoid(a0_ref[1] + pre[:, w:2 * w]) - 1.0) * ka))
    gate = _mm(jax.nn.sigmoid(xs[:, GATE_OFF:GATE_OFF + GATE_LORA]), wg_ref[...])
    ones_h = _head_ones(w, RWKV_HEAD)
    inv_n = 1.0 / RWKV_HEAD
    y = y_ref[0, 0] + y_ref[1, 0]
    mean = _mm_exact_rhs(y, ones_h) * inv_n
    yc = y - mean
    var = _mm_exact_rhs(yc * yc, ones_h) * inv_n
    yn = yc * lax.rsqrt(var + LNX_EPS) * lw_ref[...] + lb_ref[...]
    bonus = _mm_exact_rhs(r * bonus_k * rk_ref[...], ones_h) * v
    o_rwkv = _mm((yn + bonus) * gate, wro_ref[...])
    o_mla = _mm_tn(ot_ref[0], wmo_ref[...])
    pg = pg_ref[0].astype(F32)
    merged = jax.nn.sigmoid(pg[:, 0:D_MODEL]) * o_mla + jax.nn.sigmoid(pg[:, D_MODEL:2 * D_MODEL]) * o_rwkv
    g1 = mod_ref[0, :, 2 * D_MODEL:3 * D_MODEL]
    o_ref[0] = x_ref[0] + g1 * _mm(merged, wo_ref[...])


def _mix_out(x, mod3, o_t, y, xs, p_gate, w_iclr, a0, k_a, r_k, w_gate, lnx_w, lnx_b,
             w_mla_o, w_rwkv_o, w_out, tm):
    bsz, t, _ = x.shape
    w = RWKV_WIDTH
    const2 = lambda b, i: (0, 0)
    vec = lambda n: pl.BlockSpec((1, n), const2)
    blocks = (2 * _nbytes((tm, D_MODEL), F32) + _nbytes((MLA_WIDTH, tm), BF16) + 2 * _nbytes((tm, w), F32)
              + _nbytes((tm, RWKV_IN), F32) + _nbytes((tm, GATE_IN), BF16)
              + _nbytes((LANES, 2 * w), F32) + _nbytes((GATE_LORA, w), BF16)
              + 2 * _nbytes((w, D_MODEL), BF16) + _nbytes((D_MODEL, D_MODEL), BF16))
    return pl.pallas_call(
        _mix_kernel,
        grid=(bsz, t // tm),
        in_specs=[
            pl.BlockSpec((1, tm, D_MODEL), lambda b, i: (b, i, 0)),
            pl.BlockSpec((1, 1, 6 * D_MODEL), lambda b, i: (b, 0, 0)),
            pl.BlockSpec((1, MLA_WIDTH, tm), lambda b, i: (b, 0, i)),
            pl.BlockSpec((2, 1, tm, w), lambda b, i: (0, b, i, 0)),
            pl.BlockSpec((1, tm, RWKV_IN), lambda b, i: (b, i, 0)),
            pl.BlockSpec((1, tm, GATE_IN), lambda b, i: (b, i, 0)),
            pl.BlockSpec((LANES, 2 * w), const2),
            pl.BlockSpec((2, 1, w), lambda b, i: (0, 0, 0)),
            vec(w), vec(w),
            pl.BlockSpec((GATE_LORA, w), const2),
            vec(w), vec(w),
            pl.BlockSpec((MLA_WIDTH, D_MODEL), const2),
            pl.BlockSpec((w, D_MODEL), const2),
            pl.BlockSpec((D_MODEL, D_MODEL), const2),
        ],
        out_specs=pl.BlockSpec((1, tm, D_MODEL), lambda b, i: (b, i, 0)),
        out_shape=jax.ShapeDtypeStruct((bsz, t, D_MODEL), F32),
        compiler_params=pltpu.CompilerParams(
            dimension_semantics=("parallel", "parallel"),
            vmem_limit_bytes=_vmem_limit(blocks, 10 * _nbytes((tm, RWKV_IN), F32))),
        name="mix_out",
    )(x, mod3, o_t, y, xs, p_gate, w_iclr, a0, k_a, r_k, w_gate, lnx_w, lnx_b, w_mla_o, w_rwkv_o, w_out)


def _ffn_kernel(x_ref, mod_ref, nw_ref, wi_ref, wo_ref, fn_ref, o_ref, acc_ref, *, tf):
    x = x_ref[0]
    sh = mod_ref[0, :, 3 * D_MODEL:4 * D_MODEL]
    sc = mod_ref[0, :, 4 * D_MODEL:5 * D_MODEL]
    g2 = mod_ref[0, :, 5 * D_MODEL:6 * D_MODEL]
    h = (_rmsnorm(x, nw_ref[...]) * (1.0 + sc) + sh).astype(BF16)
    for j in range(D_FF // tf):
        u = _mm(h, wi_ref[:, j * tf:(j + 1) * tf])
        zg = _mm(h, wi_ref[:, D_FF + j * tf:D_FF + (j + 1) * tf])
        part = _mm(u * jax.nn.sigmoid(u) * zg, wo_ref[j * tf:(j + 1) * tf, :])
        if j == 0:
            acc_ref[...] = part
        else:
            acc_ref[...] += part
    o_ref[0] = _rmsnorm(x + g2 * acc_ref[...], fn_ref[...])


def _ffn(x, mod3, norm_w, w_in, w_out, final_w, tm, tf):
    bsz, t, _ = x.shape
    assert D_FF % tf == 0 and tf % LANES == 0
    blocks = (2 * _nbytes((tm, D_MODEL), F32) + _nbytes((D_MODEL, 2 * D_FF), BF16)
              + _nbytes((D_FF, D_MODEL), BF16))
    const2 = lambda b, i: (0, 0)
    return pl.pallas_call(
        functools.partial(_ffn_kernel, tf=tf),
        grid=(bsz, t // tm),
        in_specs=[
            pl.BlockSpec((1, tm, D_MODEL), lambda b, i: (b, i, 0)),
            pl.BlockSpec((1, 1, 6 * D_MODEL), lambda b, i: (b, 0, 0)),
            pl.BlockSpec((1, D_MODEL), const2),
            pl.BlockSpec((D_MODEL, 2 * D_FF), const2),
            pl.BlockSpec((D_FF, D_MODEL), const2),
            pl.BlockSpec((1, D_MODEL), const2),
        ],
        out_specs=pl.BlockSpec((1, tm, D_MODEL), lambda b, i: (b, i, 0)),
        out_shape=jax.ShapeDtypeStruct((bsz, t, D_MODEL), F32),
        scratch_shapes=[pltpu.VMEM((tm, D_MODEL), F32)],
        compiler_params=pltpu.CompilerParams(
            dimension_semantics=("parallel", "parallel"),
            vmem_limit_bytes=_vmem_limit(blocks, _nbytes((tm, D_MODEL), F32) + 6 * _nbytes((tm, tf), F32))),
        name="ffn",
    )(x, mod3, norm_w, w_in, w_out, final_w)


def _rope_tables(t):
    half = QK_ROPE // 2
    inv = ROPE_BASE ** (-jnp.arange(half, dtype=F32) / half)
    ang = jnp.arange(t, dtype=F32)[:, None] * inv[None, :]
    cos, sin = jnp.cos(ang), jnp.sin(ang)
    z = lambda n: jnp.zeros((t, n), F32)
    rc = jnp.concatenate([jnp.ones((t, QK_NOPE), F32), cos, cos, z(LANES - QK_NOPE - QK_ROPE)], 1)
    rs1 = jnp.concatenate([z(QK_NOPE + half), sin, z(LANES - QK_NOPE - QK_ROPE)], 1)
    rs2 = jnp.concatenate([z(QK_NOPE), -sin, z(LANES - QK_NOPE - half)], 1)
    return rc, rs1, rs2


def _prepare_params(w_in, w_uq, w_ukv, w_decay_up, w_iclr_up, w_gate_up, w_mla_o, w_rwkv_o, w_out,
                    w_ffn_in, w_ffn_out):
    dqk = QK_NOPE + QK_ROPE
    kpe_tile = jnp.zeros((D_MODEL, LANES), F32).at[:, QK_NOPE:QK_NOPE + QK_ROPE].set(
        w_in[:, Q_LORA + KV_LORA:MLA_IN])
    w_in_p = jnp.concatenate([w_in[:, 0:Q_LORA + KV_LORA], kpe_tile, w_in[:, MLA_IN:]], 1).astype(BF16)
    wq = w_uq.reshape(Q_LORA, MLA_HEADS, dqk)
    wq_p = jnp.pad(wq, ((0, 0), (0, 0), (0, HEAD_PAD - dqk))).reshape(Q_LORA, MLA_HEADS * HEAD_PAD).astype(BF16)
    wkv = w_ukv.reshape(KV_LORA, MLA_HEADS, QK_NOPE + V_HEAD)
    wk_p = jnp.pad(wkv[:, :, 0:QK_NOPE], ((0, 0), (0, 0), (0, HEAD_PAD - QK_NOPE))).reshape(
        KV_LORA, MLA_HEADS * HEAD_PAD).astype(BF16)
    wvt = wkv[:, :, QK_NOPE:].reshape(KV_LORA, MLA_WIDTH).T.astype(BF16)
    zl = jnp.zeros((DECAY_LORA, RWKV_WIDTH), F32)
    w_lora = jnp.stack([jnp.concatenate([jnp.concatenate([w_decay_up[d], zl], 1),
                                         jnp.concatenate([zl, w_iclr_up[d]], 1)], 0) for d in range(2)])
    w_iclr = jnp.concatenate([jnp.zeros((DECAY_LORA, 2 * RWKV_WIDTH), F32),
                              jnp.concatenate([w_iclr_up[0], w_iclr_up[1]], 1)], 0)
    return dict(w_in_p=w_in_p, wq_p=wq_p, wk_p=wk_p, wvt=wvt, w_lora=w_lora, w_iclr=w_iclr,
                w_gate=w_gate_up.astype(BF16), w_mla_o=w_mla_o.astype(BF16), w_rwkv_o=w_rwkv_o.astype(BF16),
                w_out=w_out.astype(BF16), w_ffn_in=w_ffn_in.astype(BF16), w_ffn_out=w_ffn_out.astype(BF16))


def _tiles(t):
    return dict(tm_proj=min(512, t), tm_prep=min(1024, t), tq=min(2048, t), tk=min(1024, t),
                rows_scan=min(512, t), tm_mix=min(512, t), tm_ffn=min(512, t), tf=256)


def _encoder(x, mod, pp, norm_mix, q_a_norm, kv_a_norm, mu_shift, w0, a0, k_k, k_a, r_k, lnx_w, lnx_b,
             norm_ffn, final_norm):
    bsz, t, _ = x.shape
    ts = _tiles(t)
    row = lambda v: v.reshape(1, -1)
    mod3 = mod.reshape(bsz, 1, 6 * D_MODEL)
    p_mla, xs, p_gate = _in_proj(x, mod3, row(norm_mix), row(mu_shift), pp["w_in_p"], ts["tm_proj"])
    q, k, vt = _mla_prep(p_mla, row(q_a_norm), row(kv_a_norm), pp["wq_p"], pp["wk_p"], pp["wvt"],
                         *_rope_tables(t), ts["tm_prep"], ts["tk"])
    o_t = _attention(q, k, vt, ts["tq"])
    y = _rwkv_scan(xs, pp["w_lora"], w0.reshape(2, 1, -1), a0.reshape(2, 1, -1), row(k_k), row(k_a),
                   ts["rows_scan"])
    x1 = _mix_out(x, mod3, o_t, y, xs, p_gate, pp["w_iclr"], a0.reshape(2, 1, -1), row(k_a),
                  row(r_k), pp["w_gate"], row(lnx_w), row(lnx_b), pp["w_mla_o"], pp["w_rwkv_o"], pp["w_out"],
                  ts["tm_mix"])
    return _ffn(x1, mod3, row(norm_ffn), pp["w_ffn_in"], pp["w_ffn_out"], row(final_norm), ts["tm_ffn"], ts["tf"])


def kernel(x_prompt, x_sample, c_prompt, c_sample, w_ada, b_ada, norm_mix, w_in, q_a_norm, kv_a_norm, w_uq, w_ukv, mu_shift, w0, w_decay_up, a0, w_iclr_up, w_gate_up, k_k, k_a, r_k, lnx_w, lnx_b, w_mla_o, w_rwkv_o, w_out, norm_ffn, w_ffn_in, w_ffn_out, final_norm):
    pp = _prepare_params(w_in[0], w_uq[0], w_ukv[0], w_decay_up[0], w_iclr_up[0], w_gate_up[0], w_mla_o[0],
                         w_rwkv_o[0], w_out[0], w_ffn_in[0], w_ffn_out[0])
    nb = x_prompt.shape[0]
    mod = _adaln_mod(jnp.concatenate([c_prompt, c_sample], 0), w_ada[0], b_ada[0])
    args = (pp, norm_mix[0], q_a_norm[0], kv_a_norm[0], mu_shift[0], w0[0], a0[0], k_k[0], k_a[0], r_k[0],
            lnx_w[0], lnx_b[0], norm_ffn[0], final_norm)
    return (_encoder(x_prompt, mod[:nb], *args), _encoder(x_sample, mod[nb:], *args))
```

```python
import functools

import jax
import jax.numpy as jnp
from jax import lax
from jax.experimental import pallas as pl
from jax.experimental.pallas import tpu as pltpu

F32 = jnp.float32
BF16 = jnp.bfloat16

D_MODEL = 1024
MLA_HEADS = 8
QK_NOPE = 64
QK_ROPE = 32
V_HEAD = 64
V_ROWS = 80
Q_LORA = 384
KV_LORA = 256
MLA_WIDTH = MLA_HEADS * V_HEAD
ROPE_BASE = 10000.0
RWKV_HEADS = 8
RWKV_HEAD = 64
RWKV_WIDTH = RWKV_HEADS * RWKV_HEAD
DECAY_LORA = 64
ICLR_LORA = 64
GATE_LORA = 128
D_FF = 2816
EPS = 1e-6
LNX_EPS = 64e-5
MLA_IN = Q_LORA + KV_LORA + QK_ROPE
RWKV_IN = 3 * RWKV_WIDTH + DECAY_LORA + ICLR_LORA + GATE_LORA
GATE_IN = 2 * D_MODEL

LANES = 128
SUBLANES = 8
HEAD_PAD = LANES
MLA_IN_PAD = Q_LORA + KV_LORA + LANES
IN_COLS_PAD = MLA_IN_PAD + RWKV_IN + GATE_IN
LORA_OFF = 3 * RWKV_WIDTH
GATE_OFF = LORA_OFF + DECAY_LORA + ICLR_LORA
VMEM_PHYS_BYTES = 64 * 1024 * 1024
VMEM_CAP_BYTES = 60000 * 1024
LOG2E = 1.4426950408889634
CHUNK = 64


def _vmem_limit(block_bytes, temp_bytes):
    return int(min(2 * block_bytes + temp_bytes, VMEM_CAP_BYTES))


def _nbytes(shape, dtype):
    n = 1
    for s in shape:
        n *= s
    return n * jnp.dtype(dtype).itemsize


def _bf(x):
    return x if x.dtype == BF16 else x.astype(BF16)


def _mm(a, b):
    return jnp.dot(_bf(a), _bf(b), preferred_element_type=F32)


def _mm_nt(a, b):
    return lax.dot_general(_bf(a), _bf(b), (((1,), (1,)), ((), ())), preferred_element_type=F32)


def _mm_tn(a, b):
    return lax.dot_general(_bf(a), _bf(b), (((0,), (0,)), ((), ())), preferred_element_type=F32)


def _split2(x):
    hi = x.astype(BF16)
    lo = (x - hi.astype(F32)).astype(BF16)
    return hi, lo


def _mm3(a, b):
    ah, al = _split2(a)
    bh, bl = _split2(b)
    return _mm(ah, bh) + (_mm(ah, bl) + _mm(al, bh))


def _mm_exact_lhs(a_bf, b):
    b0, b1 = _split2(b)
    return _mm(a_bf, b0) + _mm(a_bf, b1)


def _mm_exact_rhs(a, b_bf):
    a0, a1 = _split2(a)
    return _mm(a0, b_bf) + _mm(a1, b_bf)


def _head_ones(n, head):
    ri = lax.broadcasted_iota(jnp.int32, (n, n), 0) // head
    ci = lax.broadcasted_iota(jnp.int32, (n, n), 1) // head
    return jnp.where(ri == ci, 1.0, 0.0).astype(BF16)


def _rmsnorm(x, g):
    return x * lax.rsqrt(jnp.mean(x * x, axis=-1, keepdims=True) + EPS) * g


def _softplus(x):
    return jnp.maximum(x, 0.0) + jnp.log(1.0 + jnp.exp(-jnp.abs(x)))


def _mod_kernel(c_ref, w_ref, b_ref, o_ref):
    c = c_ref[...]
    o_ref[...] = _mm3(c * jax.nn.sigmoid(c), w_ref[...]) + b_ref[...]


def _adaln_mod(c_all, w_ada, b_ada):
    rows, n = c_all.shape[0], w_ada.shape[1]
    tn = 1536
    blocks = _nbytes((rows, D_MODEL), F32) + _nbytes((D_MODEL, tn), F32) + _nbytes((rows + 1, tn), F32)
    return pl.pallas_call(
        _mod_kernel,
        grid=(n // tn,),
        in_specs=[
            pl.BlockSpec((rows, D_MODEL), lambda j: (0, 0)),
            pl.BlockSpec((D_MODEL, tn), lambda j: (0, j)),
            pl.BlockSpec((1, tn), lambda j: (0, j)),
        ],
        out_specs=pl.BlockSpec((rows, tn), lambda j: (0, j)),
        out_shape=jax.ShapeDtypeStruct((rows, n), F32),
        compiler_params=pltpu.CompilerParams(
            dimension_semantics=("arbitrary",),
            vmem_limit_bytes=_vmem_limit(blocks, 3 * _nbytes((D_MODEL, tn), F32))),
        name="adaln_mod",
    )(c_all, w_ada, b_ada.reshape(1, n))


def _inproj_kernel(x_ref, xp_ref, xn_ref, mod_ref, nw_ref, mu_ref, w_ref, pm_ref, xs_ref, pg_ref):
    i = pl.program_id(1)
    tm = x_ref.shape[1]
    sh = mod_ref[0, :, 0:D_MODEL]
    sc = mod_ref[0, :, D_MODEL:2 * D_MODEL]
    x_ext = jnp.concatenate([xp_ref[0], x_ref[0], xn_ref[0]], 0)
    h_f32 = _rmsnorm(x_ext, nw_ref[...]) * (1.0 + sc) + sh
    h_ext = h_f32.astype(BF16)
    h = h_f32[SUBLANES:SUBLANES + tm].astype(BF16)
    a, b = MLA_IN_PAD, MLA_IN_PAD + RWKV_IN
    pm_ref[0] = _mm(h, w_ref[:, 0:a]).astype(BF16)
    pg_ref[0] = _mm(h, w_ref[:, b:IN_COLS_PAD]).astype(BF16)
    p = _mm(h_ext, w_ref[:, a:b])
    rid = lax.broadcasted_iota(jnp.int32, (tm + 2 * SUBLANES, 1), 0)
    outside = ((rid < SUBLANES) & (i == 0)) | ((rid >= tm + SUBLANES) & (i == pl.num_programs(1) - 1))
    p = jnp.where(outside, 0.0, p)
    n = tm + 2 * SUBLANES
    shifted = 0.5 * (pltpu.roll(p, 1, 0) + pltpu.roll(p, n - 1, 0))
    xs = p + mu_ref[...] * (shifted - p)
    xs_ref[0] = xs[SUBLANES:SUBLANES + tm]


def _in_proj(x, mod3, norm_w, mu, w_in_p, tm):
    bsz, t, _ = x.shape
    per = tm // SUBLANES
    last = t // SUBLANES - 1
    blocks = (_nbytes((tm + 2 * SUBLANES, D_MODEL), F32) + _nbytes((D_MODEL, IN_COLS_PAD), BF16)
              + _nbytes((tm, MLA_IN_PAD), BF16) + _nbytes((tm, RWKV_IN), F32) + _nbytes((tm, GATE_IN), BF16))
    return pl.pallas_call(
        _inproj_kernel,
        grid=(bsz, t // tm),
        in_specs=[
            pl.BlockSpec((1, tm, D_MODEL), lambda b, i: (b, i, 0)),
            pl.BlockSpec((1, SUBLANES, D_MODEL), lambda b, i: (b, jnp.maximum(i * per - 1, 0), 0)),
            pl.BlockSpec((1, SUBLANES, D_MODEL), lambda b, i: (b, jnp.minimum((i + 1) * per, last), 0)),
            pl.BlockSpec((1, 1, 6 * D_MODEL), lambda b, i: (b, 0, 0)),
            pl.BlockSpec((1, D_MODEL), lambda b, i: (0, 0)),
            pl.BlockSpec((1, RWKV_IN), lambda b, i: (0, 0)),
            pl.BlockSpec((D_MODEL, IN_COLS_PAD), lambda b, i: (0, 0)),
        ],
        out_specs=[
            pl.BlockSpec((1, tm, MLA_IN_PAD), lambda b, i: (b, i, 0)),
            pl.BlockSpec((1, tm, RWKV_IN), lambda b, i: (b, i, 0)),
            pl.BlockSpec((1, tm, GATE_IN), lambda b, i: (b, i, 0)),
        ],
        out_shape=[
            jax.ShapeDtypeStruct((bsz, t, MLA_IN_PAD), BF16),
            jax.ShapeDtypeStruct((bsz, t, RWKV_IN), F32),
            jax.ShapeDtypeStruct((bsz, t, GATE_IN), BF16),
        ],
        compiler_params=pltpu.CompilerParams(
            dimension_semantics=("parallel", "parallel"),
            vmem_limit_bytes=_vmem_limit(blocks, 2 * _nbytes((tm, IN_COLS_PAD), F32))),
        name="in_proj",
    )(x, x, x, mod3, norm_w, mu, w_in_p)


def _mla_prep_kernel(pm_ref, qn_ref, kvn_ref, wq_ref, wk_ref, wvt_ref, rc_ref, rs1_ref, rs2_ref,
                     q_ref, k_ref, vt_ref, *, tk):
    p = pm_ref[0].astype(F32)
    cq = _rmsnorm(p[:, 0:Q_LORA], qn_ref[...]).astype(BF16)
    ckv = _rmsnorm(p[:, Q_LORA:Q_LORA + KV_LORA], kvn_ref[...]).astype(BF16)
    rc, rs1, rs2 = rc_ref[...], rs1_ref[...], rs2_ref[...]

    def rope(x):
        return x * rc + pltpu.roll(x, QK_ROPE // 2, 1) * rs1 + pltpu.roll(x, LANES - QK_ROPE // 2, 1) * rs2

    k_pe = rope(p[:, Q_LORA + KV_LORA:MLA_IN_PAD])
    q = _mm(cq, wq_ref[...])
    kn = _mm(ckv, wk_ref[...])
    vt = _mm_nt(wvt_ref[...], ckv)
    scale = (QK_NOPE + QK_ROPE) ** -0.5 * LOG2E
    tail = jnp.where(lax.broadcasted_iota(jnp.int32, (V_ROWS - V_HEAD, tk), 0) == 0, 1.0, 0.0)
    for h in range(MLA_HEADS):
        sl = slice(h * HEAD_PAD, (h + 1) * HEAD_PAD)
        q_ref[0, h] = (rope(q[:, sl]) * scale).astype(BF16)
        k_ref[0, h] = (kn[:, sl] + k_pe).astype(BF16)
        for c in range(vt.shape[1] // tk):
            vt_ref[0, h, c] = jnp.concatenate(
                [vt[h * V_HEAD:(h + 1) * V_HEAD, c * tk:(c + 1) * tk], tail], 0).astype(BF16)


def _mla_prep(p_mla, q_norm, kv_norm, wq_p, wk_p, wvt, rope_c, rope_s1, rope_s2, tm, tk):
    bsz, t, _ = p_mla.shape
    assert tm % tk == 0
    hp = MLA_HEADS * HEAD_PAD
    blocks = (_nbytes((tm, MLA_IN_PAD), BF16) + _nbytes((Q_LORA + KV_LORA, hp), BF16)
              + _nbytes((MLA_WIDTH, KV_LORA), BF16) + 3 * _nbytes((tm, LANES), F32)
              + 2 * _nbytes((tm, hp), BF16) + _nbytes((MLA_WIDTH, tm), BF16))
    const = lambda b, i: (0, 0)
    return pl.pallas_call(
        functools.partial(_mla_prep_kernel, tk=tk),
        grid=(bsz, t // tm),
        in_specs=[
            pl.BlockSpec((1, tm, MLA_IN_PAD), lambda b, i: (b, i, 0)),
            pl.BlockSpec((1, Q_LORA), const),
            pl.BlockSpec((1, KV_LORA), const),
            pl.BlockSpec((Q_LORA, hp), const),
            pl.BlockSpec((KV_LORA, hp), const),
            pl.BlockSpec((MLA_WIDTH, KV_LORA), const),
            pl.BlockSpec((tm, LANES), lambda b, i: (i, 0)),
            pl.BlockSpec((tm, LANES), lambda b, i: (i, 0)),
            pl.BlockSpec((tm, LANES), lambda b, i: (i, 0)),
        ],
        out_specs=[
            pl.BlockSpec((1, MLA_HEADS, tm, HEAD_PAD), lambda b, i: (b, 0, i, 0)),
            pl.BlockSpec((1, MLA_HEADS, tm, HEAD_PAD), lambda b, i: (b, 0, i, 0)),
            pl.BlockSpec((1, MLA_HEADS, tm // tk, V_ROWS, tk), lambda b, i: (b, 0, i, 0, 0)),
        ],
        out_shape=[
            jax.ShapeDtypeStruct((bsz, MLA_HEADS, t, HEAD_PAD), BF16),
            jax.ShapeDtypeStruct((bsz, MLA_HEADS, t, HEAD_PAD), BF16),
            jax.ShapeDtypeStruct((bsz, MLA_HEADS, t // tk, V_ROWS, tk), BF16),
        ],
        compiler_params=pltpu.CompilerParams(
            dimension_semantics=("parallel", "parallel"),
            vmem_limit_bytes=_vmem_limit(blocks, 6 * _nbytes((tm, hp), F32))),
        name="mla_prep",
    )(p_mla, q_norm, kv_norm, wq_p, wk_p, wvt, rope_c, rope_s1, rope_s2)


def _attn_kernel(q_ref, k_ref, vt_ref, o_ref):
    q = q_ref[0, 0]
    tq = q.shape[0]
    nk, tk = vt_ref.shape[2], vt_ref.shape[4]

    def body(j, carry):
        m, acc = carry
        r0 = pl.multiple_of(j * tk, tk)
        st = _mm_nt(k_ref[0, 0, pl.ds(r0, tk), :], q)
        m_new = jnp.maximum(m, jnp.max(st, axis=0, keepdims=True))
        p = jnp.exp2(st - m_new)
        acc = jnp.exp2(m - m_new) * acc + _mm(vt_ref[0, 0, j], p)
        return m_new, acc

    init = (jnp.full((1, tq), -jnp.inf, F32), jnp.zeros((V_ROWS, tq), F32))
    _, acc = lax.fori_loop(0, nk, body, init, unroll=min(nk, 8))
    o_ref[0] = (acc[0:V_HEAD] / acc[V_HEAD:V_HEAD + 1]).astype(BF16)


def _attention(q, k, vt, tq):
    bsz, nh, t, _ = q.shape
    nk, tk = vt.shape[2], vt.shape[4]
    blocks = (_nbytes((tq, HEAD_PAD), BF16) + _nbytes((t, HEAD_PAD), BF16) + _nbytes((V_ROWS, t), BF16)
              + _nbytes((V_HEAD, tq), BF16))
    return pl.pallas_call(
        _attn_kernel,
        grid=(bsz, nh, t // tq),
        in_specs=[
            pl.BlockSpec((1, 1, tq, HEAD_PAD), lambda b, h, i: (b, h, i, 0)),
            pl.BlockSpec((1, 1, t, HEAD_PAD), lambda b, h, i: (b, h, 0, 0)),
            pl.BlockSpec((1, 1, nk, V_ROWS, tk), lambda b, h, i: (b, h, 0, 0, 0)),
        ],
        out_specs=pl.BlockSpec((1, V_HEAD, tq), lambda b, h, i: (b, h, i)),
        out_shape=jax.ShapeDtypeStruct((bsz, nh * V_HEAD, t), BF16),
        compiler_params=pltpu.CompilerParams(
            dimension_semantics=("parallel", "parallel", "arbitrary"),
            vmem_limit_bytes=_vmem_limit(blocks, 6 * _nbytes((tk, tq), F32))),
        name="mla_attn",
    )(q, k, vt)


def _rwkv_scan_kernel(xs_ref, wl_ref, w0_ref, a0_ref, kk_ref, ka_ref, y_ref,
                      r_sc, v_sc, kn_sc, lw_sc, a_sc, kd_sc, h_sc, lhs_sc, yl_sc, nt_sc, gam_sc, *, rows):
    d = pl.program_id(0)
    i = pl.program_id(2)
    c = CHUNK
    w = RWKV_WIDTH

    @pl.when(i == 0)
    def _():
        h_sc[...] = jnp.zeros_like(h_sc)

    xs = xs_ref[0]
    k = xs[:, w:2 * w]
    z = xs[:, LORA_OFF:LORA_OFF + LANES]
    lane = lax.broadcasted_iota(jnp.int32, z.shape, 1)
    pre = _mm3(jnp.where(lane < DECAY_LORA, jnp.tanh(z), z), wl_ref[0])
    wlog = -_softplus(-(w0_ref[0] + pre[:, 0:w])) - 0.5
    a = jax.nn.sigmoid(a0_ref[0] + pre[:, w:2 * w])
    kn = k * kk_ref[...]
    ones_h = _head_ones(w, RWKV_HEAD)
    kn = kn * lax.rsqrt(_mm_exact_rhs(kn * kn, ones_h) + 1e-12)
    r_sc[...] = xs[:, 0:w]
    v_sc[...] = xs[:, 2 * w:3 * w]
    kn_sc[...] = kn
    lw_sc[...] = -jnp.exp(wlog)
    a_sc[...] = a
    kd_sc[...] = k * (1.0 + (a - 1.0) * ka_ref[...])

    ri = lax.broadcasted_iota(jnp.int32, (2 * c, 2 * c), 0)
    ci = lax.broadcasted_iota(jnp.int32, (2 * c, 2 * c), 1)
    rt, cs = ri % c, ci % c
    before = (rt - cs) * (1 - 2 * d) > 0
    keep = before | ((ri >= c) & (cs == rt))
    rt_c = lax.broadcasted_iota(jnp.int32, (c, c), 0)
    cs_c = lax.broadcasted_iota(jnp.int32, (c, c), 1)
    tri = jnp.where((rt_c - cs_c) * (1 - 2 * d) >= 0, 1.0, 0.0).astype(BF16)
    eye = ri == ci
    lane_c = lax.broadcasted_iota(jnp.int32, (c, LANES), 1)
    lo_half = lane_c < RWKV_HEAD
    zeros_c = jnp.zeros((c, LANES), F32)
    nsub = rows // c

    def head_lo(x, h):
        s = x[:, (h // 2) * LANES:(h // 2 + 1) * LANES]
        if h % 2:
            s = pltpu.roll(s, RWKV_HEAD, 1)
        return jnp.where(lo_half[0:x.shape[0]], s, 0.0)

    def head_hi(x, h):
        s = x[:, (h // 2) * LANES:(h // 2 + 1) * LANES]
        if h % 2 == 0:
            s = pltpu.roll(s, RWKV_HEAD, 1)
        return jnp.where(lo_half[0:x.shape[0]], 0.0, s)

    chains = [(s, h) for s in range(nsub) for h in range(RWKV_HEADS)]
    rh, vh, w_bot, pw, x, f, e_hat, g_hat = {}, {}, {}, {}, {}, {}, {}, {}
    for s in range(nsub):
        sl = slice(s * c, (s + 1) * c)
        lw = lw_sc[sl, :]
        cum = _mm_exact_lhs(tri, lw)
        tot = jnp.sum(lw, axis=0, keepdims=True)
        kn_c, a_c, kd_c = kn_sc[sl, :], a_sc[sl, :], kd_sc[sl, :]
        e_neg = jnp.exp(-cum)
        e_end = jnp.exp(tot - cum)
        r_t = r_sc[sl, :] * jnp.exp(cum)
        a_t = -kn_c * jnp.exp(cum - lw)
        b_t = kn_c * a_c * e_neg
        k_t = kd_c * e_neg
        b_e = kn_c * a_c * e_end
        k_e = kd_c * e_end
        g_end = jnp.exp(tot)
        v_c = v_sc[sl, :]
        for h in range(RWKV_HEADS):
            a_lo = head_lo(a_t, h)
            rh[s, h], vh[s, h] = head_lo(r_t, h), head_hi(v_c, h)
            g = _mm_nt(jnp.concatenate([a_lo, rh[s, h]], 0),
                       jnp.concatenate([head_lo(b_t, h), head_lo(k_t, h)], 0))
            g = jnp.where(keep, g, 0.0)
            w_bot[s, h] = g[c:2 * c]
            pw[s, h] = jnp.where(lo_half, g[0:c], 0.0)
            x[s, h] = (a_lo, jnp.where(lo_half, 0.0, g[0:c]))
            g_hat[s, h] = head_lo(g_end, h)
        for j in range(RWKV_HEADS // 2):
            e_hat[s, j] = jnp.concatenate([b_e[:, j * LANES:(j + 1) * LANES], k_e[:, j * LANES:(j + 1) * LANES]], 0)
    for ch in chains:
        a_lo, l_ak = x[ch]
        x[ch] = a_lo + _mm(l_ak, jnp.concatenate([zeros_c, vh[ch]], 0))
    zeros_xp = jnp.zeros((c, 2 * LANES), F32)
    for step in range(6):
        for ch in chains:
            if step < 5:
                z = _mm(pw[ch], jnp.concatenate([jnp.concatenate([x[ch], pw[ch]], 1), zeros_xp], 0))
                x[ch] = x[ch] + z[:, 0:LANES]
                pw[ch] = z[:, LANES:2 * LANES]
            else:
                x[ch] = x[ch] + _mm(pw[ch], jnp.concatenate([x[ch], zeros_c], 0))
    for ch in chains:
        f[ch] = jnp.concatenate([x[ch], vh[ch]], 0)
        wf = _mm(w_bot[ch], f[ch])
        lhs_sc[ch[0], ch[1], c:2 * c] = jnp.where(lo_half, rh[ch] + wf, 0.0).astype(BF16)
        yl_sc[ch[0], ch[1]] = jnp.where(lo_half, 0.0, wf)
        gcol = jnp.sum(jnp.where(eye, g_hat[ch], 0.0), axis=1, keepdims=True)[0:c]
        gam_sc[ch[0], ch[1]] = jnp.broadcast_to(gcol, (c, LANES))
    for s in range(nsub):
        for j in range(RWKV_HEADS // 2):
            mn2 = _mm_tn(e_hat[s, j], jnp.concatenate([f[s, 2 * j], f[s, 2 * j + 1]], 1))
            for h, mn in ((2 * j, mn2[0:c, 0:LANES]), (2 * j + 1, mn2[c:2 * c, LANES:2 * LANES])):
                lhs_sc[s, h, 0:c] = jnp.where(lo_half, mn, 0.0).astype(BF16)
                nt_sc[s, h] = jnp.where(lo_half, 0.0, mn)

    def advance(s, carry):
        sub = s + d * (nsub - 1 - 2 * s)
        sl = pl.ds(pl.multiple_of(sub * c, c), c)
        for j in range(RWKV_HEADS // 2):
            ys = []
            for h in (2 * j, 2 * j + 1):
                hs = h_sc[h]
                res = _mm(lhs_sc[sub, h], hs)
                h_new = gam_sc[sub, h] * hs[0:c] + res[0:c] + nt_sc[sub, h]
                h_sc[h] = jnp.concatenate([h_new, zeros_c], 0)
                ys.append(res[c:2 * c] + yl_sc[sub, h])
            y_ref[0, 0, sl, j * LANES:(j + 1) * LANES] = pltpu.roll(ys[0], RWKV_HEAD, 1) + ys[1]
        return carry

    lax.fori_loop(0, nsub, advance, 0)


def _rwkv_scan(xs, w_lora, w0, a0, k_k, k_a, rows):
    bsz, t, _ = xs.shape
    nblk = t // rows
    w = RWKV_WIDTH
    blk_of = lambda d, b, i: i + d * (nblk - 1 - 2 * i)
    vec = lambda n: pl.BlockSpec((1, n), lambda d, b, i: (0, 0))
    dvec = lambda n: pl.BlockSpec((1, 1, n), lambda d, b, i: (d, 0, 0))
    blocks = (_nbytes((rows, RWKV_IN), F32) + _nbytes((LANES, 2 * w), F32) + _nbytes((rows, w), F32))
    per_chain = (rows // CHUNK, RWKV_HEADS, CHUNK, LANES)
    lhs_rows = (rows // CHUNK, RWKV_HEADS, 2 * CHUNK, LANES)
    scratch = (6 * _nbytes((rows, w), F32) + _nbytes((RWKV_HEADS, LANES, LANES), F32)
               + 3 * _nbytes(per_chain, F32) + _nbytes(lhs_rows, BF16))
    return pl.pallas_call(
        functools.partial(_rwkv_scan_kernel, rows=rows),
        grid=(2, bsz, nblk),
        in_specs=[
            pl.BlockSpec((1, rows, RWKV_IN), lambda d, b, i: (b, blk_of(d, b, i), 0)),
            pl.BlockSpec((1, LANES, 2 * w), lambda d, b, i: (d, 0, 0)),
            dvec(w), dvec(w), vec(w), vec(w),
        ],
        out_specs=pl.BlockSpec((1, 1, rows, w), lambda d, b, i: (d, b, blk_of(d, b, i), 0)),
        out_shape=jax.ShapeDtypeStruct((2, bsz, t, w), F32),
        scratch_shapes=([pltpu.VMEM((rows, w), F32)] * 6 + [pltpu.VMEM((RWKV_HEADS, LANES, LANES), F32)]
                        + [pltpu.VMEM(lhs_rows, BF16)] + [pltpu.VMEM(per_chain, F32)] * 3),
        compiler_params=pltpu.CompilerParams(
            dimension_semantics=("parallel", "parallel", "arbitrary"),
            vmem_limit_bytes=_vmem_limit(blocks, scratch + 8 * _nbytes((rows, RWKV_IN), F32))),
        name="rwkv_scan",
    )(xs, w_lora, w0, a0, k_k, k_a)


def _mix_kernel(x_ref, mod_ref, ot_ref, y_ref, xs_ref, pg_ref, wi_ref, a0_ref, ka_ref,
                rk_ref, wg_ref, lw_ref, lb_ref, wmo_ref, wro_ref, wo_ref, o_ref):
    w = RWKV_WIDTH
    xs = xs_ref[0]
    r, k, v = xs[:, 0:w], xs[:, w:2 * w], xs[:, 2 * w:3 * w]
    pre = _mm(xs[:, LORA_OFF:LORA_OFF + LANES], wi_ref[...])
    ka = ka_ref[...]
    bonus_k = (k * (1.0 + (jax.nn.sigmoid(a0_ref[0] + pre[:, 0:w]) - 1.0) * ka)
               + k * (1.0 + (jax.nn.sigmoid(a0_ref[1] + pre[:, w:2 * w]) - 1.0) * ka))
    gate = _mm(jax.nn.sigmoid(xs[:, GATE_OFF:GATE_OFF + GATE_LORA]), wg_ref[...])
    ones_h = _head_ones(w, RWKV_HEAD)
    inv_n = 1.0 / RWKV_HEAD
    y = y_ref[0, 0] + y_ref[1, 0]
    mean = _mm_exact_rhs(y, ones_h) * inv_n
    yc = y - mean
    var = _mm_exact_rhs(yc * yc, ones_h) * inv_n
    yn = yc * lax.rsqrt(var + LNX_EPS) * lw_ref[...] + lb_ref[...]
    bonus = _mm_exact_rhs(r * bonus_k * rk_ref[...], ones_h) * v
    o_rwkv = _mm((yn + bonus) * gate, wro_ref[...])
    o_mla = _mm_tn(ot_ref[0], wmo_ref[...])
    pg = pg_ref[0].astype(F32)
    merged = jax.nn.sigmoid(pg[:, 0:D_MODEL]) * o_mla + jax.nn.sigmoid(pg[:, D_MODEL:2 * D_MODEL]) * o_rwkv
    g1 = mod_ref[0, :, 2 * D_MODEL:3 * D_MODEL]
    o_ref[0] = x_ref[0] + g1 * _mm(merged, wo_ref[...])


def _mix_out(x, mod3, o_t, y, xs, p_gate, w_iclr, a0, k_a, r_k, w_gate, lnx_w, lnx_b,
             w_mla_o, w_rwkv_o, w_out, tm):
    bsz, t, _ = x.shape
    w = RWKV_WIDTH
    const2 = lambda b, i: (0, 0)
    vec = lambda n: pl.BlockSpec((1, n), const2)
    blocks = (2 * _nbytes((tm, D_MODEL), F32) + _nbytes((MLA_WIDTH, tm), BF16) + 2 * _nbytes((tm, w), F32)
              + _nbytes((tm, RWKV_IN), F32) + _nbytes((tm, GATE_IN), BF16)
              + _nbytes((LANES, 2 * w), F32) + _nbytes((GATE_LORA, w), BF16)
              + 2 * _nbytes((w, D_MODEL), BF16) + _nbytes((D_MODEL, D_MODEL), BF16))
    return pl.pallas_call(
        _mix_kernel,
        grid=(bsz, t // tm),
        in_specs=[
            pl.BlockSpec((1, tm, D_MODEL), lambda b, i: (b, i, 0)),
            pl.BlockSpec((1, 1, 6 * D_MODEL), lambda b, i: (b, 0, 0)),
            pl.BlockSpec((1, MLA_WIDTH, tm), lambda b, i: (b, 0, i)),
            pl.BlockSpec((2, 1, tm, w), lambda b, i: (0, b, i, 0)),
            pl.BlockSpec((1, tm, RWKV_IN), lambda b, i: (b, i, 0)),
            pl.BlockSpec((1, tm, GATE_IN), lambda b, i: (b, i, 0)),
            pl.BlockSpec((LANES, 2 * w), const2),
            pl.BlockSpec((2, 1, w), lambda b, i: (0, 0, 0)),
            vec(w), vec(w),
            pl.BlockSpec((GATE_LORA, w), const2),
            vec(w), vec(w),
            pl.BlockSpec((MLA_WIDTH, D_MODEL), const2),
            pl.BlockSpec((w, D_MODEL), const2),
            pl.BlockSpec((D_MODEL, D_MODEL), const2),
        ],
        out_specs=pl.BlockSpec((1, tm, D_MODEL), lambda b, i: (b, i, 0)),
        out_shape=jax.ShapeDtypeStruct((bsz, t, D_MODEL), F32),
        compiler_params=pltpu.CompilerParams(
            dimension_semantics=("parallel", "parallel"),
            vmem_limit_bytes=_vmem_limit(blocks, 10 * _nbytes((tm, RWKV_IN), F32))),
        name="mix_out",
    )(x, mod3, o_t, y, xs, p_gate, w_iclr, a0, k_a, r_k, w_gate, lnx_w, lnx_b, w_mla_o, w_rwkv_o, w_out)


def _ffn_kernel(x_ref, mod_ref, nw_ref, wi_ref, wo_ref, fn_ref, o_ref, acc_ref, *, tf):
    x = x_ref[0]
    sh = mod_ref[0, :, 3 * D_MODEL:4 * D_MODEL]
    sc = mod_ref[0, :, 4 * D_MODEL:5 * D_MODEL]
    g2 = mod_ref[0, :, 5 * D_MODEL:6 * D_MODEL]
    h = (_rmsnorm(x, nw_ref[...]) * (1.0 + sc) + sh).astype(BF16)
    for j in range(D_FF // tf):
        u = _mm(h, wi_ref[:, j * tf:(j + 1) * tf])
        zg = _mm(h, wi_ref[:, D_FF + j * tf:D_FF + (j + 1) * tf])
        part = _mm(u * jax.nn.sigmoid(u) * zg, wo_ref[j * tf:(j + 1) * tf, :])
        if j == 0:
            acc_ref[...] = part
        else:
            acc_ref[...] += part
    o_ref[0] = _rmsnorm(x + g2 * acc_ref[...], fn_ref[...])


def _ffn(x, mod3, norm_w, w_in, w_out, final_w, tm, tf):
    bsz, t, _ = x.shape
    assert D_FF % tf == 0 and tf % LANES == 0
    blocks = (2 * _nbytes((tm, D_MODEL), F32) + _nbytes((D_MODEL, 2 * D_FF), BF16)
              + _nbytes((D_FF, D_MODEL), BF16))
    const2 = lambda b, i: (0, 0)
    return pl.pallas_call(
        functools.partial(_ffn_kernel, tf=tf),
        grid=(bsz, t // tm),
        in_specs=[
            pl.BlockSpec((1, tm, D_MODEL), lambda b, i: (b, i, 0)),
            pl.BlockSpec((1, 1, 6 * D_MODEL), lambda b, i: (b, 0, 0)),
            pl.BlockSpec((1, D_MODEL), const2),
            pl.BlockSpec((D_MODEL, 2 * D_FF), const2),
            pl.BlockSpec((D_FF, D_MODEL), const2),
            pl.BlockSpec((1, D_MODEL), const2),
        ],
        out_specs=pl.BlockSpec((1, tm, D_MODEL), lambda b, i: (b, i, 0)),
        out_shape=jax.ShapeDtypeStruct((bsz, t, D_MODEL), F32),
        scratch_shapes=[pltpu.VMEM((tm, D_MODEL), F32)],
        compiler_params=pltpu.CompilerParams(
            dimension_semantics=("parallel", "parallel"),
            vmem_limit_bytes=_vmem_limit(blocks, _nbytes((tm, D_MODEL), F32) + 6 * _nbytes((tm, tf), F32))),
        name="ffn",
    )(x, mod3, norm_w, w_in, w_out, final_w)


def _rope_tables(t):
    half = QK_ROPE // 2
    inv = ROPE_BASE ** (-jnp.arange(half, dtype=F32) / half)
    ang = jnp.arange(t, dtype=F32)[:, None] * inv[None, :]
    cos, sin = jnp.cos(ang), jnp.sin(ang)
    z = lambda n: jnp.zeros((t, n), F32)
    rc = jnp.concatenate([jnp.ones((t, QK_NOPE), F32), cos, cos, z(LANES - QK_NOPE - QK_ROPE)], 1)
    rs1 = jnp.concatenate([z(QK_NOPE + half), sin, z(LANES - QK_NOPE - QK_ROPE)], 1)
    rs2 = jnp.concatenate([z(QK_NOPE), -sin, z(LANES - QK_NOPE - half)], 1)
    return rc, rs1, rs2


def _prepare_params(w_in, w_uq, w_ukv, w_decay_up, w_iclr_up, w_gate_up, w_mla_o, w_rwkv_o, w_out,
                    w_ffn_in, w_ffn_out):
    dqk = QK_NOPE + QK_ROPE
    kpe_tile = jnp.zeros((D_MODEL, LANES), F32).at[:, QK_NOPE:QK_NOPE + QK_ROPE].set(
        w_in[:, Q_LORA + KV_LORA:MLA_IN])
    w_in_p = jnp.concatenate([w_in[:, 0:Q_LORA + KV_LORA], kpe_tile, w_in[:, MLA_IN:]], 1).astype(BF16)
    wq = w_uq.reshape(Q_LORA, MLA_HEADS, dqk)
    wq_p = jnp.pad(wq, ((0, 0), (0, 0), (0, HEAD_PAD - dqk))).reshape(Q_LORA, MLA_HEADS * HEAD_PAD).astype(BF16)
    wkv = w_ukv.reshape(KV_LORA, MLA_HEADS, QK_NOPE + V_HEAD)
    wk_p = jnp.pad(wkv[:, :, 0:QK_NOPE], ((0, 0), (0, 0), (0, HEAD_PAD - QK_NOPE))).reshape(
        KV_LORA, MLA_HEADS * HEAD_PAD).astype(BF16)
    wvt = wkv[:, :, QK_NOPE:].reshape(KV_LORA, MLA_WIDTH).T.astype(BF16)
    zl = jnp.zeros((DECAY_LORA, RWKV_WIDTH), F32)
    w_lora = jnp.stack([jnp.concatenate([jnp.concatenate([w_decay_up[d], zl], 1),
                                         jnp.concatenate([zl, w_iclr_up[d]], 1)], 0) for d in range(2)])
    w_iclr = jnp.concatenate([jnp.zeros((DECAY_LORA, 2 * RWKV_WIDTH), F32),
                              jnp.concatenate([w_iclr_up[0], w_iclr_up[1]], 1)], 0)
    return dict(w_in_p=w_in_p, wq_p=wq_p, wk_p=wk_p, wvt=wvt, w_lora=w_lora, w_iclr=w_iclr,
                w_gate=w_gate_up.astype(BF16), w_mla_o=w_mla_o.astype(BF16), w_rwkv_o=w_rwkv_o.astype(BF16),
                w_out=w_out.astype(BF16), w_ffn_in=w_ffn_in.astype(BF16), w_ffn_out=w_ffn_out.astype(BF16))


def _tiles(t):
    return dict(tm_proj=min(512, t), tm_prep=min(1024, t), tq=min(2048, t), tk=min(1024, t),
                rows_scan=min(512, t), tm_mix=min(512, t), tm_ffn=min(512, t), tf=256)


def _encoder(x, mod, pp, norm_mix, q_a_norm, kv_a_norm, mu_shift, w0, a0, k_k, k_a, r_k, lnx_w, lnx_b,
             norm_ffn, final_norm):
    bsz, t, _ = x.shape
    ts = _tiles(t)
    row = lambda v: v.reshape(1, -1)
    mod3 = mod.reshape(bsz, 1, 6 * D_MODEL)
    p_mla, xs, p_gate = _in_proj(x, mod3, row(norm_mix), row(mu_shift), pp["w_in_p"], ts["tm_proj"])
    q, k, vt = _mla_prep(p_mla, row(q_a_norm), row(kv_a_norm), pp["wq_p"], pp["wk_p"], pp["wvt"],
                         *_rope_tables(t), ts["tm_prep"], ts["tk"])
    o_t = _attention(q, k, vt, ts["tq"])
    y = _rwkv_scan(xs, pp["w_lora"], w0.reshape(2, 1, -1), a0.reshape(2, 1, -1), row(k_k), row(k_a),
                   ts["rows_scan"])
    x1 = _mix_out(x, mod3, o_t, y, xs, p_gate, pp["w_iclr"], a0.reshape(2, 1, -1), row(k_a),
                  row(r_k), pp["w_gate"], row(lnx_w), row(lnx_b), pp["w_mla_o"], pp["w_rwkv_o"], pp["w_out"],
                  ts["tm_mix"])
    return _ffn(x1, mod3, row(norm_ffn), pp["w_ffn_in"], pp["w_ffn_out"], row(final_norm), ts["tm_ffn"], ts["tf"])


def kernel(x_prompt, x_sample, c_prompt, c_sample, w_ada, b_ada, norm_mix, w_in, q_a_norm, kv_a_norm, w_uq, w_ukv, mu_shift, w0, w_decay_up, a0, w_iclr_up, w_gate_up, k_k, k_a, r_k, lnx_w, lnx_b, w_mla_o, w_rwkv_o, w_out, norm_ffn, w_ffn_in, w_ffn_out, final_norm):
    pp = _prepare_params(w_in[0], w_uq[0], w_ukv[0], w_decay_up[0], w_iclr_up[0], w_gate_up[0], w_mla_o[0],
                         w_rwkv_o[0], w_out[0], w_ffn_in[0], w_ffn_out[0])
    nb = x_prompt.shape[0]
    mod = _adaln_mod(jnp.concatenate([c_prompt, c_sample], 0), w_ada[0], b_ada[0])
    args = (pp, norm_mix[0], q_a_norm[0], kv_a_norm[0], mu_shift[0], w0[0], a0[0], k_k[0], k_a[0], r_k[0],
            lnx_w[0], lnx_b[0], norm_ffn[0], final_norm)
    return (_encoder(x_prompt, mod[:nb], *args), _encoder(x_sample, mod[nb:], *args))
```

```python
import functools

import jax
import jax.numpy as jnp
from jax import lax
from jax.experimental import pallas as pl
from jax.experimental.pallas import tpu as pltpu

F32 = jnp.float32
BF16 = jnp.bfloat16

D_MODEL = 1024
MLA_HEADS = 8
QK_NOPE = 64
QK_ROPE = 32
V_HEAD = 64
V_ROWS = 80
Q_LORA = 384
KV_LORA = 256
MLA_WIDTH = MLA_HEADS * V_HEAD
ROPE_BASE = 10000.0
RWKV_HEADS = 8
RWKV_HEAD = 64
RWKV_WIDTH = RWKV_HEADS * RWKV_HEAD
DECAY_LORA = 64
ICLR_LORA = 64
GATE_LORA = 128
D_FF = 2816
EPS = 1e-6
LNX_EPS = 64e-5
MLA_IN = Q_LORA + KV_LORA + QK_ROPE
RWKV_IN = 3 * RWKV_WIDTH + DECAY_LORA + ICLR_LORA + GATE_LORA
GATE_IN = 2 * D_MODEL

LANES = 128
SUBLANES = 8
HEAD_PAD = LANES
MLA_IN_PAD = Q_LORA + KV_LORA + LANES
IN_COLS_PAD = MLA_IN_PAD + RWKV_IN + GATE_IN
LORA_OFF = 3 * RWKV_WIDTH
GATE_OFF = LORA_OFF + DECAY_LORA + ICLR_LORA
VMEM_PHYS_BYTES = 64 * 1024 * 1024
VMEM_CAP_BYTES = 60000 * 1024
LOG2E = 1.4426950408889634
CHUNK = 64


def _vmem_limit(block_bytes, temp_bytes):
    return int(min(2 * block_bytes + temp_bytes, VMEM_CAP_BYTES))


def _nbytes(shape, dtype):
    n = 1
    for s in shape:
        n *= s
    return n * jnp.dtype(dtype).itemsize


def _bf(x):
    return x if x.dtype == BF16 else x.astype(BF16)


def _mm(a, b):
    return jnp.dot(_bf(a), _bf(b), preferred_element_type=F32)


def _mm_nt(a, b):
    return lax.dot_general(_bf(a), _bf(b), (((1,), (1,)), ((), ())), preferred_element_type=F32)


def _mm_tn(a, b):
    return lax.dot_general(_bf(a), _bf(b), (((0,), (0,)), ((), ())), preferred_element_type=F32)


def _split2(x):
    hi = x.astype(BF16)
    lo = (x - hi.astype(F32)).astype(BF16)
    return hi, lo


def _mm3(a, b):
    ah, al = _split2(a)
    bh, bl = _split2(b)
    return _mm(ah, bh) + (_mm(ah, bl) + _mm(al, bh))


def _mm_exact_lhs(a_bf, b):
    b0, b1 = _split2(b)
    return _mm(a_bf, b0) + _mm(a_bf, b1)


def _head_ones(n, head):
    ri = lax.broadcasted_iota(jnp.int32, (n, n), 0) // head
    ci = lax.broadcasted_iota(jnp.int32, (n, n), 1) // head
    return jnp.where(ri == ci, 1.0, 0.0).astype(BF16)


def _rmsnorm(x, g):
    return x * lax.rsqrt(jnp.mean(x * x, axis=-1, keepdims=True) + EPS) * g


def _softplus(x):
    return jnp.maximum(x, 0.0) + jnp.log(1.0 + jnp.exp(-jnp.abs(x)))


def _mod_kernel(c_ref, w_ref, b_ref, o_ref):
    c = c_ref[...]
    o_ref[...] = _mm3(c * jax.nn.sigmoid(c), w_ref[...]) + b_ref[...]


def _adaln_mod(c_all, w_ada, b_ada):
    rows, n = c_all.shape[0], w_ada.shape[1]
    tn = 1536
    blocks = _nbytes((rows, D_MODEL), F32) + _nbytes((D_MODEL, tn), F32) + _nbytes((rows + 1, tn), F32)
    return pl.pallas_call(
        _mod_kernel,
        grid=(n // tn,),
        in_specs=[
            pl.BlockSpec((rows, D_MODEL), lambda j: (0, 0)),
            pl.BlockSpec((D_MODEL, tn), lambda j: (0, j)),
            pl.BlockSpec((1, tn), lambda j: (0, j)),
        ],
        out_specs=pl.BlockSpec((rows, tn), lambda j: (0, j)),
        out_shape=jax.ShapeDtypeStruct((rows, n), F32),
        compiler_params=pltpu.CompilerParams(
            dimension_semantics=("arbitrary",),
            vmem_limit_bytes=_vmem_limit(blocks, 3 * _nbytes((D_MODEL, tn), F32))),
        name="adaln_mod",
    )(c_all, w_ada, b_ada.reshape(1, n))


def _inproj_kernel(x_ref, xp_ref, xn_ref, mod_ref, nw_ref, mu_ref, w_ref, pm_ref, xs_ref, pg_ref):
    i = pl.program_id(1)
    tm = x_ref.shape[1]
    sh = mod_ref[0, :, 0:D_MODEL]
    sc = mod_ref[0, :, D_MODEL:2 * D_MODEL]
    x_ext = jnp.concatenate([xp_ref[0], x_ref[0], xn_ref[0]], 0)
    h_f32 = _rmsnorm(x_ext, nw_ref[...]) * (1.0 + sc) + sh
    h_ext = h_f32.astype(BF16)
    h = h_f32[SUBLANES:SUBLANES + tm].astype(BF16)
    a, b = MLA_IN_PAD, MLA_IN_PAD + RWKV_IN
    pm_ref[0] = _mm(h, w_ref[:, 0:a]).astype(BF16)
    pg_ref[0] = _mm(h, w_ref[:, b:IN_COLS_PAD]).astype(BF16)
    p = _mm(h_ext, w_ref[:, a:b])
    rid = lax.broadcasted_iota(jnp.int32, (tm + 2 * SUBLANES, 1), 0)
    outside = ((rid < SUBLANES) & (i == 0)) | ((rid >= tm + SUBLANES) & (i == pl.num_programs(1) - 1))
    p = jnp.where(outside, 0.0, p)
    n = tm + 2 * SUBLANES
    shifted = 0.5 * (pltpu.roll(p, 1, 0) + pltpu.roll(p, n - 1, 0))
    xs = p + mu_ref[...] * (shifted - p)
    xs_ref[0] = xs[SUBLANES:SUBLANES + tm]


def _in_proj(x, mod3, norm_w, mu, w_in_p, tm):
    bsz, t, _ = x.shape
    per = tm // SUBLANES
    last = t // SUBLANES - 1
    blocks = (_nbytes((tm + 2 * SUBLANES, D_MODEL), F32) + _nbytes((D_MODEL, IN_COLS_PAD), BF16)
              + _nbytes((tm, MLA_IN_PAD), BF16) + _nbytes((tm, RWKV_IN), F32) + _nbytes((tm, GATE_IN), BF16))
    return pl.pallas_call(
        _inproj_kernel,
        grid=(bsz, t // tm),
        in_specs=[
            pl.BlockSpec((1, tm, D_MODEL), lambda b, i: (b, i, 0)),
            pl.BlockSpec((1, SUBLANES, D_MODEL), lambda b, i: (b, jnp.maximum(i * per - 1, 0), 0)),
            pl.BlockSpec((1, SUBLANES, D_MODEL), lambda b, i: (b, jnp.minimum((i + 1) * per, last), 0)),
            pl.BlockSpec((1, 1, 6 * D_MODEL), lambda b, i: (b, 0, 0)),
            pl.BlockSpec((1, D_MODEL), lambda b, i: (0, 0)),
            pl.BlockSpec((1, RWKV_IN), lambda b, i: (0, 0)),
            pl.BlockSpec((D_MODEL, IN_COLS_PAD), lambda b, i: (0, 0)),
        ],
        out_specs=[
            pl.BlockSpec((1, tm, MLA_IN_PAD), lambda b, i: (b, i, 0)),
            pl.BlockSpec((1, tm, RWKV_IN), lambda b, i: (b, i, 0)),
            pl.BlockSpec((1, tm, GATE_IN), lambda b, i: (b, i, 0)),
        ],
        out_shape=[
            jax.ShapeDtypeStruct((bsz, t, MLA_IN_PAD), BF16),
            jax.ShapeDtypeStruct((bsz, t, RWKV_IN), F32),
            jax.ShapeDtypeStruct((bsz, t, GATE_IN), BF16),
        ],
        compiler_params=pltpu.CompilerParams(
            dimension_semantics=("parallel", "parallel"),
            vmem_limit_bytes=_vmem_limit(blocks, 2 * _nbytes((tm, IN_COLS_PAD), F32))),
        name="in_proj",
    )(x, x, x, mod3, norm_w, mu, w_in_p)


def _mla_prep_kernel(pm_ref, qn_ref, kvn_ref, wq_ref, wk_ref, wvt_ref, rc_ref, rs1_ref, rs2_ref,
                     q_ref, k_ref, vt_ref, *, tk):
    p = pm_ref[0].astype(F32)
    cq = _rmsnorm(p[:, 0:Q_LORA], qn_ref[...]).astype(BF16)
    ckv = _rmsnorm(p[:, Q_LORA:Q_LORA + KV_LORA], kvn_ref[...]).astype(BF16)
    rc, rs1, rs2 = rc_ref[...], rs1_ref[...], rs2_ref[...]

    def rope(x):
        return x * rc + pltpu.roll(x, QK_ROPE // 2, 1) * rs1 + pltpu.roll(x, LANES - QK_ROPE // 2, 1) * rs2

    k_pe = rope(p[:, Q_LORA + KV_LORA:MLA_IN_PAD])
    q = _mm(cq, wq_ref[...])
    kn = _mm(ckv, wk_ref[...])
    vt = _mm_nt(wvt_ref[...], ckv)
    scale = (QK_NOPE + QK_ROPE) ** -0.5 * LOG2E
    tail = jnp.where(lax.broadcasted_iota(jnp.int32, (V_ROWS - V_HEAD, tk), 0) == 0, 1.0, 0.0)
    for h in range(MLA_HEADS):
        sl = slice(h * HEAD_PAD, (h + 1) * HEAD_PAD)
        q_ref[0, h] = (rope(q[:, sl]) * scale).astype(BF16)
        k_ref[0, h] = (kn[:, sl] + k_pe).astype(BF16)
        for c in range(vt.shape[1] // tk):
            vt_ref[0, h, c] = jnp.concatenate(
                [vt[h * V_HEAD:(h + 1) * V_HEAD, c * tk:(c + 1) * tk], tail], 0).astype(BF16)


def _mla_prep(p_mla, q_norm, kv_norm, wq_p, wk_p, wvt, rope_c, rope_s1, rope_s2, tm, tk):
    bsz, t, _ = p_mla.shape
    assert tm % tk == 0
    hp = MLA_HEADS * HEAD_PAD
    blocks = (_nbytes((tm, MLA_IN_PAD), BF16) + _nbytes((Q_LORA + KV_LORA, hp), BF16)
              + _nbytes((MLA_WIDTH, KV_LORA), BF16) + 3 * _nbytes((tm, LANES), F32)
              + 2 * _nbytes((tm, hp), BF16) + _nbytes((MLA_WIDTH, tm), BF16))
    const = lambda b, i: (0, 0)
    return pl.pallas_call(
        functools.partial(_mla_prep_kernel, tk=tk),
        grid=(bsz, t // tm),
        in_specs=[
            pl.BlockSpec((1, tm, MLA_IN_PAD), lambda b, i: (b, i, 0)),
            pl.BlockSpec((1, Q_LORA), const),
            pl.BlockSpec((1, KV_LORA), const),
            pl.BlockSpec((Q_LORA, hp), const),
            pl.BlockSpec((KV_LORA, hp), const),
            pl.BlockSpec((MLA_WIDTH, KV_LORA), const),
            pl.BlockSpec((tm, LANES), lambda b, i: (i, 0)),
            pl.BlockSpec((tm, LANES), lambda b, i: (i, 0)),
            pl.BlockSpec((tm, LANES), lambda b, i: (i, 0)),
        ],
        out_specs=[
            pl.BlockSpec((1, MLA_HEADS, tm, HEAD_PAD), lambda b, i: (b, 0, i, 0)),
            pl.BlockSpec((1, MLA_HEADS, tm, HEAD_PAD), lambda b, i: (b, 0, i, 0)),
            pl.BlockSpec((1, MLA_HEADS, tm // tk, V_ROWS, tk), lambda b, i: (b, 0, i, 0, 0)),
        ],
        out_shape=[
            jax.ShapeDtypeStruct((bsz, MLA_HEADS, t, HEAD_PAD), BF16),
            jax.ShapeDtypeStruct((bsz, MLA_HEADS, t, HEAD_PAD), BF16),
            jax.ShapeDtypeStruct((bsz, MLA_HEADS, t // tk, V_ROWS, tk), BF16),
        ],
        compiler_params=pltpu.CompilerParams(
            dimension_semantics=("parallel", "parallel"),
            vmem_limit_bytes=_vmem_limit(blocks, 6 * _nbytes((tm, hp), F32))),
        name="mla_prep",
    )(p_mla, q_norm, kv_norm, wq_p, wk_p, wvt, rope_c, rope_s1, rope_s2)


def _attn_kernel(q_ref, k_ref, vt_ref, o_ref):
    q = q_ref[0, 0]
    tq = q.shape[0]
    nk, tk = vt_ref.shape[2], vt_ref.shape[4]

    def body(j, carry):
        m, acc = carry
        r0 = pl.multiple_of(j * tk, tk)
        st = _mm_nt(k_ref[0, 0, pl.ds(r0, tk), :], q)
        m_new = jnp.maximum(m, jnp.max(st, axis=0, keepdims=True))
        p = jnp.exp2(st - m_new)
        acc = jnp.exp2(m - m_new) * acc + _mm(vt_ref[0, 0, j], p)
        return m_new, acc

    init = (jnp.full((1, tq), -jnp.inf, F32), jnp.zeros((V_ROWS, tq), F32))
    _, acc = lax.fori_loop(0, nk, body, init, unroll=min(nk, 8))
    o_ref[0] = (acc[0:V_HEAD] / acc[V_HEAD:V_HEAD + 1]).astype(BF16)


def _attention(q, k, vt, tq):
    bsz, nh, t, _ = q.shape
    nk, tk = vt.shape[2], vt.shape[4]
    blocks = (_nbytes((tq, HEAD_PAD), BF16) + _nbytes((t, HEAD_PAD), BF16) + _nbytes((V_ROWS, t), BF16)
              + _nbytes((V_HEAD, tq), BF16))
    return pl.pallas_call(
        _attn_kernel,
        grid=(bsz, nh, t // tq),
        in_specs=[
            pl.BlockSpec((1, 1, tq, HEAD_PAD), lambda b, h, i: (b, h, i, 0)),
            pl.BlockSpec((1, 1, t, HEAD_PAD), lambda b, h, i: (b, h, 0, 0)),
            pl.BlockSpec((1, 1, nk, V_ROWS, tk), lambda b, h, i: (b, h, 0, 0, 0)),
        ],
        out_specs=pl.BlockSpec((1, V_HEAD, tq), lambda b, h, i: (b, h, i)),
        out_shape=jax.ShapeDtypeStruct((bsz, nh * V_HEAD, t), BF16),
        compiler_params=pltpu.CompilerParams(
            dimension_semantics=("parallel", "parallel", "arbitrary"),
            vmem_limit_bytes=_vmem_limit(blocks, 6 * _nbytes((tk, tq), F32))),
        name="mla_attn",
    )(q, k, vt)


def _rwkv_scan_kernel(xs_ref, wl_ref, w0_ref, a0_ref, kk_ref, ka_ref, y_ref,
                      r_sc, v_sc, kn_sc, lw_sc, a_sc, kd_sc, h_sc, lhs_sc, yl_sc, nt_sc, gam_sc, *, rows):
    d = pl.program_id(0)
    i = pl.program_id(2)
    c = CHUNK
    w = RWKV_WIDTH

    @pl.when(i == 0)
    def _():
        h_sc[...] = jnp.zeros_like(h_sc)

    xs = xs_ref[0]
    k = xs[:, w:2 * w]
    z = xs[:, LORA_OFF:LORA_OFF + LANES]
    lane = lax.broadcasted_iota(jnp.int32, z.shape, 1)
    pre = _mm(jnp.where(lane < DECAY_LORA, jnp.tanh(z), z), wl_ref[0])
    wlog = -_softplus(-(w0_ref[0] + pre[:, 0:w])) - 0.5
    a = jax.nn.sigmoid(a0_ref[0] + pre[:, w:2 * w])
    kn = k * kk_ref[...]
    ones_h = _head_ones(w, RWKV_HEAD)
    kn = kn * lax.rsqrt(_mm(kn * kn, ones_h) + 1e-12)
    r_sc[...] = xs[:, 0:w]
    v_sc[...] = xs[:, 2 * w:3 * w]
    kn_sc[...] = kn
    lw_sc[...] = -jnp.exp(wlog)
    a_sc[...] = a
    kd_sc[...] = k * (1.0 + (a - 1.0) * ka_ref[...])

    ri = lax.broadcasted_iota(jnp.int32, (2 * c, 2 * c), 0)
    ci = lax.broadcasted_iota(jnp.int32, (2 * c, 2 * c), 1)
    rt, cs = ri % c, ci % c
    before = (rt - cs) * (1 - 2 * d) > 0
    keep = before | ((ri >= c) & (cs == rt))
    rt_c = lax.broadcasted_iota(jnp.int32, (c, c), 0)
    cs_c = lax.broadcasted_iota(jnp.int32, (c, c), 1)
    tri = jnp.where((rt_c - cs_c) * (1 - 2 * d) >= 0, 1.0, 0.0).astype(BF16)
    eye = ri == ci
    lane_c = lax.broadcasted_iota(jnp.int32, (c, LANES), 1)
    lo_half = lane_c < RWKV_HEAD
    zeros_c = jnp.zeros((c, LANES), F32)
    nsub = rows // c

    def head_lo(x, h):
        s = x[:, (h // 2) * LANES:(h // 2 + 1) * LANES]
        if h % 2:
            s = pltpu.roll(s, RWKV_HEAD, 1)
        return jnp.where(lo_half[0:x.shape[0]], s, 0.0)

    def head_hi(x, h):
        s = x[:, (h // 2) * LANES:(h // 2 + 1) * LANES]
        if h % 2 == 0:
            s = pltpu.roll(s, RWKV_HEAD, 1)
        return jnp.where(lo_half[0:x.shape[0]], 0.0, s)

    chains = [(s, h) for s in range(nsub) for h in range(RWKV_HEADS)]
    rh, vh, w_bot, pw, x, f, e_hat, g_hat = {}, {}, {}, {}, {}, {}, {}, {}
    for s in range(nsub):
        sl = slice(s * c, (s + 1) * c)
        lw = lw_sc[sl, :]
        cum = _mm_exact_lhs(tri, lw)
        tot = jnp.sum(lw, axis=0, keepdims=True)
        kn_c, a_c, kd_c = kn_sc[sl, :], a_sc[sl, :], kd_sc[sl, :]
        e_neg = jnp.exp(-cum)
        e_end = jnp.exp(tot - cum)
        r_t = r_sc[sl, :] * jnp.exp(cum)
        a_t = -kn_c * jnp.exp(cum - lw)
        b_t = kn_c * a_c * e_neg
        k_t = kd_c * e_neg
        b_e = kn_c * a_c * e_end
        k_e = kd_c * e_end
        g_end = jnp.exp(tot)
        v_c = v_sc[sl, :]
        for h in range(RWKV_HEADS):
            a_lo = head_lo(a_t, h)
            rh[s, h], vh[s, h] = head_lo(r_t, h), head_hi(v_c, h)
            g = _mm_nt(jnp.concatenate([a_lo, rh[s, h]], 0),
                       jnp.concatenate([head_lo(b_t, h), head_lo(k_t, h)], 0))
            g = jnp.where(keep, g, 0.0)
            w_bot[s, h] = g[c:2 * c]
            pw[s, h] = jnp.where(lo_half, g[0:c], 0.0)
            x[s, h] = (a_lo, jnp.where(lo_half, 0.0, g[0:c]))
            g_hat[s, h] = head_lo(g_end, h)
        for j in range(RWKV_HEADS // 2):
            e_hat[s, j] = jnp.concatenate([b_e[:, j * LANES:(j + 1) * LANES], k_e[:, j * LANES:(j + 1) * LANES]], 0)
    for ch in chains:
        a_lo, l_ak = x[ch]
        x[ch] = a_lo + _mm(l_ak, jnp.concatenate([zeros_c, vh[ch]], 0))
    zeros_xp = jnp.zeros((c, 2 * LANES), F32)
    for step in range(6):
        for ch in chains:
            if step < 5:
                z = _mm(pw[ch], jnp.concatenate([jnp.concatenate([x[ch], pw[ch]], 1), zeros_xp], 0))
                x[ch] = x[ch] + z[:, 0:LANES]
                pw[ch] = z[:, LANES:2 * LANES]
            else:
                x[ch] = x[ch] + _mm(pw[ch], jnp.concatenate([x[ch], zeros_c], 0))
    for ch in chains:
        f[ch] = jnp.concatenate([x[ch], vh[ch]], 0)
        wf = _mm(w_bot[ch], f[ch])
        lhs_sc[ch[0], ch[1], c:2 * c] = jnp.where(lo_half, rh[ch] + wf, 0.0).astype(BF16)
        yl_sc[ch[0], ch[1]] = jnp.where(lo_half, 0.0, wf)
        gcol = jnp.sum(jnp.where(eye, g_hat[ch], 0.0), axis=1, keepdims=True)[0:c]
        gam_sc[ch[0], ch[1]] = jnp.broadcast_to(gcol, (c, LANES))
    for s in range(nsub):
        for j in range(RWKV_HEADS // 2):
            mn2 = _mm_tn(e_hat[s, j], jnp.concatenate([f[s, 2 * j], f[s, 2 * j + 1]], 1))
            for h, mn in ((2 * j, mn2[0:c, 0:LANES]), (2 * j + 1, mn2[c:2 * c, LANES:2 * LANES])):
                lhs_sc[s, h, 0:c] = jnp.where(lo_half, mn, 0.0).astype(BF16)
                nt_sc[s, h] = jnp.where(lo_half, 0.0, mn)

    def advance(s, carry):
        sub = s + d * (nsub - 1 - 2 * s)
        sl = pl.ds(pl.multiple_of(sub * c, c), c)
        for j in range(RWKV_HEADS // 2):
            ys = []
            for h in (2 * j, 2 * j + 1):
                hs = h_sc[h]
                res = _mm(lhs_sc[sub, h], hs)
                h_new = gam_sc[sub, h] * hs[0:c] + res[0:c] + nt_sc[sub, h]
                h_sc[h] = jnp.concatenate([h_new, zeros_c], 0)
                ys.append(res[c:2 * c] + yl_sc[sub, h])
            y_ref[0, 0, sl, j * LANES:(j + 1) * LANES] = pltpu.roll(ys[0], RWKV_HEAD, 1) + ys[1]
        return carry

    lax.fori_loop(0, nsub, advance, 0)


def _rwkv_scan(xs, w_lora, w0, a0, k_k, k_a, rows):
    bsz, t, _ = xs.shape
    nblk = t // rows
    w = RWKV_WIDTH
    blk_of = lambda d, b, i: i + d * (nblk - 1 - 2 * i)
    vec = lambda n: pl.BlockSpec((1, n), lambda d, b, i: (0, 0))
    dvec = lambda n: pl.BlockSpec((1, 1, n), lambda d, b, i: (d, 0, 0))
    blocks = (_nbytes((rows, RWKV_IN), F32) + _nbytes((LANES, 2 * w), F32) + _nbytes((rows, w), F32))
    per_chain = (rows // CHUNK, RWKV_HEADS, CHUNK, LANES)
    lhs_rows = (rows // CHUNK, RWKV_HEADS, 2 * CHUNK, LANES)
    scratch = (6 * _nbytes((rows, w), F32) + _nbytes((RWKV_HEADS, LANES, LANES), F32)
               + 3 * _nbytes(per_chain, F32) + _nbytes(lhs_rows, BF16))
    return pl.pallas_call(
        functools.partial(_rwkv_scan_kernel, rows=rows),
        grid=(2, bsz, nblk),
        in_specs=[
            pl.BlockSpec((1, rows, RWKV_IN), lambda d, b, i: (b, blk_of(d, b, i), 0)),
            pl.BlockSpec((1, LANES, 2 * w), lambda d, b, i: (d, 0, 0)),
            dvec(w), dvec(w), vec(w), vec(w),
        ],
        out_specs=pl.BlockSpec((1, 1, rows, w), lambda d, b, i: (d, b, blk_of(d, b, i), 0)),
        out_shape=jax.ShapeDtypeStruct((2, bsz, t, w), F32),
        scratch_shapes=([pltpu.VMEM((rows, w), F32)] * 6 + [pltpu.VMEM((RWKV_HEADS, LANES, LANES), F32)]
                        + [pltpu.VMEM(lhs_rows, BF16)] + [pltpu.VMEM(per_chain, F32)] * 3),
        compiler_params=pltpu.CompilerParams(
            dimension_semantics=("parallel", "parallel", "arbitrary"),
            vmem_limit_bytes=_vmem_limit(blocks, scratch + 8 * _nbytes((rows, RWKV_IN), F32))),
        name="rwkv_scan",
    )(xs, w_lora, w0, a0, k_k, k_a)


def _mix_kernel(x_ref, mod_ref, ot_ref, y_ref, xs_ref, pg_ref, wi_ref, a0_ref, ka_ref,
                rk_ref, wg_ref, lw_ref, lb_ref, wmo_ref, wro_ref, wo_ref, o_ref):
    w = RWKV_WIDTH
    xs = xs_ref[0]
    r, k, v = xs[:, 0:w], xs[:, w:2 * w], xs[:, 2 * w:3 * w]
    pre = _mm(xs[:, LORA_OFF:LORA_OFF + LANES], wi_ref[...])
    ka = ka_ref[...]
    bonus_k = (k * (1.0 + (jax.nn.sigmoid(a0_ref[0] + pre[:, 0:w]) - 1.0) * ka)
               + k * (1.0 + (jax.nn.sigmoid(a0_ref[1] + pre[:, w:2 * w]) - 1.0) * ka))
    gate = _mm(jax.nn.sigmoid(xs[:, GATE_OFF:GATE_OFF + GATE_LORA]), wg_ref[...])
    ones_h = _head_ones(w, RWKV_HEAD)
    inv_n = 1.0 / RWKV_HEAD
    y = y_ref[0, 0] + y_ref[1, 0]
    mean = _mm(y, ones_h) * inv_n
    yc = y - mean
    var = _mm(yc * yc, ones_h) * inv_n
    yn = yc * lax.rsqrt(var + LNX_EPS) * lw_ref[...] + lb_ref[...]
    bonus = _mm(r * bonus_k * rk_ref[...], ones_h) * v
    o_rwkv = _mm((yn + bonus) * gate, wro_ref[...])
    o_mla = _mm_tn(ot_ref[0], wmo_ref[...])
    pg = pg_ref[0].astype(F32)
    merged = jax.nn.sigmoid(pg[:, 0:D_MODEL]) * o_mla + jax.nn.sigmoid(pg[:, D_MODEL:2 * D_MODEL]) * o_rwkv
    g1 = mod_ref[0, :, 2 * D_MODEL:3 * D_MODEL]
    o_ref[0] = x_ref[0] + g1 * _mm(merged, wo_ref[...])


def _mix_out(x, mod3, o_t, y, xs, p_gate, w_iclr, a0, k_a, r_k, w_gate, lnx_w, lnx_b,
             w_mla_o, w_rwkv_o, w_out, tm):
    bsz, t, _ = x.shape
    w = RWKV_WIDTH
    const2 = lambda b, i: (0, 0)
    vec = lambda n: pl.BlockSpec((1, n), const2)
    blocks = (2 * _nbytes((tm, D_MODEL), F32) + _nbytes((MLA_WIDTH, tm), BF16) + 2 * _nbytes((tm, w), F32)
              + _nbytes((tm, RWKV_IN), F32) + _nbytes((tm, GATE_IN), BF16)
              + _nbytes((LANES, 2 * w), F32) + _nbytes((GATE_LORA, w), BF16)
              + 2 * _nbytes((w, D_MODEL), BF16) + _nbytes((D_MODEL, D_MODEL), BF16))
    return pl.pallas_call(
        _mix_kernel,
        grid=(bsz, t // tm),
        in_specs=[
            pl.BlockSpec((1, tm, D_MODEL), lambda b, i: (b, i, 0)),
            pl.BlockSpec((1, 1, 6 * D_MODEL), lambda b, i: (b, 0, 0)),
            pl.BlockSpec((1, MLA_WIDTH, tm), lambda b, i: (b, 0, i)),
            pl.BlockSpec((2, 1, tm, w), lambda b, i: (0, b, i, 0)),
            pl.BlockSpec((1, tm, RWKV_IN), lambda b, i: (b, i, 0)),
            pl.BlockSpec((1, tm, GATE_IN), lambda b, i: (b, i, 0)),
            pl.BlockSpec((LANES, 2 * w), const2),
            pl.BlockSpec((2, 1, w), lambda b, i: (0, 0, 0)),
            vec(w), vec(w),
            pl.BlockSpec((GATE_LORA, w), const2),
            vec(w), vec(w),
            pl.BlockSpec((MLA_WIDTH, D_MODEL), const2),
            pl.BlockSpec((w, D_MODEL), const2),
            pl.BlockSpec((D_MODEL, D_MODEL), const2),
        ],
        out_specs=pl.BlockSpec((1, tm, D_MODEL), lambda b, i: (b, i, 0)),
        out_shape=jax.ShapeDtypeStruct((bsz, t, D_MODEL), F32),
        compiler_params=pltpu.CompilerParams(
            dimension_semantics=("parallel", "parallel"),
            vmem_limit_bytes=_vmem_limit(blocks, 10 * _nbytes((tm, RWKV_IN), F32))),
        name="mix_out",
    )(x, mod3, o_t, y, xs, p_gate, w_iclr, a0, k_a, r_k, w_gate, lnx_w, lnx_b, w_mla_o, w_rwkv_o, w_out)


def _ffn_kernel(x_ref, mod_ref, nw_ref, wi_ref, wo_ref, fn_ref, o_ref, acc_ref, *, tf):
    x = x_ref[0]
    sh = mod_ref[0, :, 3 * D_MODEL:4 * D_MODEL]
    sc = mod_ref[0, :, 4 * D_MODEL:5 * D_MODEL]
    g2 = mod_ref[0, :, 5 * D_MODEL:6 * D_MODEL]
    h = (_rmsnorm(x, nw_ref[...]) * (1.0 + sc) + sh).astype(BF16)
    for j in range(D_FF // tf):
        u = _mm(h, wi_ref[:, j * tf:(j + 1) * tf])
        zg = _mm(h, wi_ref[:, D_FF + j * tf:D_FF + (j + 1) * tf])
        part = _mm(u * jax.nn.sigmoid(u) * zg, wo_ref[j * tf:(j + 1) * tf, :])
        if j == 0:
            acc_ref[...] = part
        else:
            acc_ref[...] += part
    o_ref[0] = _rmsnorm(x + g2 * acc_ref[...], fn_ref[...])


def _ffn(x, mod3, norm_w, w_in, w_out, final_w, tm, tf):
    bsz, t, _ = x.shape
    assert D_FF % tf == 0 and tf % LANES == 0
    blocks = (2 * _nbytes((tm, D_MODEL), F32) + _nbytes((D_MODEL, 2 * D_FF), BF16)
              + _nbytes((D_FF, D_MODEL), BF16))
    const2 = lambda b, i: (0, 0)
    return pl.pallas_call(
        functools.partial(_ffn_kernel, tf=tf),
        grid=(bsz, t // tm),
        in_specs=[
            pl.BlockSpec((1, tm, D_MODEL), lambda b, i: (b, i, 0)),
            pl.BlockSpec((1, 1, 6 * D_MODEL), lambda b, i: (b, 0, 0)),
            pl.BlockSpec((1, D_MODEL), const2),
            pl.BlockSpec((D_MODEL, 2 * D_FF), const2),
            pl.BlockSpec((D_FF, D_MODEL), const2),
            pl.BlockSpec((1, D_MODEL), const2),
        ],
        out_specs=pl.BlockSpec((1, tm, D_MODEL), lambda b, i: (b, i, 0)),
        out_shape=jax.ShapeDtypeStruct((bsz, t, D_MODEL), F32),
        scratch_shapes=[pltpu.VMEM((tm, D_MODEL), F32)],
        compiler_params=pltpu.CompilerParams(
            dimension_semantics=("parallel", "parallel"),
            vmem_limit_bytes=_vmem_limit(blocks, _nbytes((tm, D_MODEL), F32) + 6 * _nbytes((tm, tf), F32))),
        name="ffn",
    )(x, mod3, norm_w, w_in, w_out, final_w)


def _rope_tables(t):
    half = QK_ROPE // 2
    inv = ROPE_BASE ** (-jnp.arange(half, dtype=F32) / half)
    ang = jnp.arange(t, dtype=F32)[:, None] * inv[None, :]
    cos, sin = jnp.cos(ang), jnp.sin(ang)
    z = lambda n: jnp.zeros((t, n), F32)
    rc = jnp.concatenate([jnp.ones((t, QK_NOPE), F32), cos, cos, z(LANES - QK_NOPE - QK_ROPE)], 1)
    rs1 = jnp.concatenate([z(QK_NOPE + half), sin, z(LANES - QK_NOPE - QK_ROPE)], 1)
    rs2 = jnp.concatenate([z(QK_NOPE), -sin, z(LANES - QK_NOPE - half)], 1)
    return rc, rs1, rs2


def _prepare_params(w_in, w_uq, w_ukv, w_decay_up, w_iclr_up, w_gate_up, w_mla_o, w_rwkv_o, w_out,
                    w_ffn_in, w_ffn_out):
    dqk = QK_NOPE + QK_ROPE
    kpe_tile = jnp.zeros((D_MODEL, LANES), F32).at[:, QK_NOPE:QK_NOPE + QK_ROPE].set(
        w_in[:, Q_LORA + KV_LORA:MLA_IN])
    w_in_p = jnp.concatenate([w_in[:, 0:Q_LORA + KV_LORA], kpe_tile, w_in[:, MLA_IN:]], 1).astype(BF16)
    wq = w_uq.reshape(Q_LORA, MLA_HEADS, dqk)
    wq_p = jnp.pad(wq, ((0, 0), (0, 0), (0, HEAD_PAD - dqk))).reshape(Q_LORA, MLA_HEADS * HEAD_PAD).astype(BF16)
    wkv = w_ukv.reshape(KV_LORA, MLA_HEADS, QK_NOPE + V_HEAD)
    wk_p = jnp.pad(wkv[:, :, 0:QK_NOPE], ((0, 0), (0, 0), (0, HEAD_PAD - QK_NOPE))).reshape(
        KV_LORA, MLA_HEADS * HEAD_PAD).astype(BF16)
    wvt = wkv[:, :, QK_NOPE:].reshape(KV_LORA, MLA_WIDTH).T.astype(BF16)
    zl = jnp.zeros((DECAY_LORA, RWKV_WIDTH), F32)
    w_lora = jnp.stack([jnp.concatenate([jnp.concatenate([w_decay_up[d], zl], 1),
                                         jnp.concatenate([zl, w_iclr_up[d]], 1)], 0) for d in range(2)])
    w_iclr = jnp.concatenate([jnp.zeros((DECAY_LORA, 2 * RWKV_WIDTH), F32),
                              jnp.concatenate([w_iclr_up[0], w_iclr_up[1]], 1)], 0)
    return dict(w_in_p=w_in_p, wq_p=wq_p, wk_p=wk_p, wvt=wvt, w_lora=w_lora, w_iclr=w_iclr,
                w_gate=w_gate_up.astype(BF16), w_mla_o=w_mla_o.astype(BF16), w_rwkv_o=w_rwkv_o.astype(BF16),
                w_out=w_out.astype(BF16), w_ffn_in=w_ffn_in.astype(BF16), w_ffn_out=w_ffn_out.astype(BF16))


def _tiles(t):
    return dict(tm_proj=min(512, t), tm_prep=min(1024, t), tq=min(2048, t), tk=min(1024, t),
                rows_scan=min(512, t), tm_mix=min(512, t), tm_ffn=min(512, t), tf=256)


def _encoder(x, mod, pp, norm_mix, q_a_norm, kv_a_norm, mu_shift, w0, a0, k_k, k_a, r_k, lnx_w, lnx_b,
             norm_ffn, final_norm):
    bsz, t, _ = x.shape
    ts = _tiles(t)
    row = lambda v: v.reshape(1, -1)
    mod3 = mod.reshape(bsz, 1, 6 * D_MODEL)
    p_mla, xs, p_gate = _in_proj(x, mod3, row(norm_mix), row(mu_shift), pp["w_in_p"], ts["tm_proj"])
    q, k, vt = _mla_prep(p_mla, row(q_a_norm), row(kv_a_norm), pp["wq_p"], pp["wk_p"], pp["wvt"],
                         *_rope_tables(t), ts["tm_prep"], ts["tk"])
    o_t = _attention(q, k, vt, ts["tq"])
    y = _rwkv_scan(xs, pp["w_lora"], w0.reshape(2, 1, -1), a0.reshape(2, 1, -1), row(k_k), row(k_a),
                   ts["rows_scan"])
    x1 = _mix_out(x, mod3, o_t, y, xs, p_gate, pp["w_iclr"], a0.reshape(2, 1, -1), row(k_a),
                  row(r_k), pp["w_gate"], row(lnx_w), row(lnx_b), pp["w_mla_o"], pp["w_rwkv_o"], pp["w_out"],
                  ts["tm_mix"])
    return _ffn(x1, mod3, row(norm_ffn), pp["w_ffn_in"], pp["w_ffn_out"], row(final_norm), ts["tm_ffn"], ts["tf"])


def kernel(x_prompt, x_sample, c_prompt, c_sample, w_ada, b_ada, norm_mix, w_in, q_a_norm, kv_a_norm, w_uq, w_ukv, mu_shift, w0, w_decay_up, a0, w_iclr_up, w_gate_up, k_k, k_a, r_k, lnx_w, lnx_b, w_mla_o, w_rwkv_o, w_out, norm_ffn, w_ffn_in, w_ffn_out, final_norm):
    pp = _prepare_params(w_in[0], w_uq[0], w_ukv[0], w_decay_up[0], w_iclr_up[0], w_gate_up[0], w_mla_o[0],
                         w_rwkv_o[0], w_out[0], w_ffn_in[0], w_ffn_out[0])
    nb = x_prompt.shape[0]
    mod = _adaln_mod(jnp.concatenate([c_prompt, c_sample], 0), w_ada[0], b_ada[0])
    args = (pp, norm_mix[0], q_a_norm[0], kv_a_norm[0], mu_shift[0], w0[0], a0[0], k_k[0], k_a[0], r_k[0],
            lnx_w[0], lnx_b[0], norm_ffn[0], final_norm)
    return (_encoder(x_prompt, mod[:nb], *args), _encoder(x_sample, mod[nb:], *args))
```

```python
import functools

import jax
import jax.numpy as jnp
from jax import lax
from jax.experimental import pallas as pl
from jax.experimental.pallas import tpu as pltpu

F32 = jnp.float32
BF16 = jnp.bfloat16

D_MODEL = 1024
MLA_HEADS = 8
QK_NOPE = 64
QK_ROPE = 32
V_HEAD = 64
V_ROWS = 80
Q_LORA = 384
KV_LORA = 256
MLA_WIDTH = MLA_HEADS * V_HEAD
ROPE_BASE = 10000.0
RWKV_HEADS = 8
RWKV_HEAD = 64
RWKV_WIDTH = RWKV_HEADS * RWKV_HEAD
DECAY_LORA = 64
ICLR_LORA = 64
GATE_LORA = 128
D_FF = 2816
EPS = 1e-6
LNX_EPS = 64e-5
MLA_IN = Q_LORA + KV_LORA + QK_ROPE
RWKV_IN = 3 * RWKV_WIDTH + DECAY_LORA + ICLR_LORA + GATE_LORA
GATE_IN = 2 * D_MODEL

LANES = 128
SUBLANES = 8
HEAD_PAD = LANES
MLA_IN_PAD = Q_LORA + KV_LORA + LANES
IN_COLS_PAD = MLA_IN_PAD + RWKV_IN + GATE_IN
LORA_OFF = 3 * RWKV_WIDTH
GATE_OFF = LORA_OFF + DECAY_LORA + ICLR_LORA
VMEM_CAP_BYTES = 60000 * 1024
LOG2E = 1.4426950408889634
CHUNK = 64


def _vmem_limit(block_bytes, temp_bytes):
    return int(min(2 * block_bytes + temp_bytes, VMEM_CAP_BYTES))


def _nbytes(shape, dtype):
    n = 1
    for s in shape:
        n *= s
    return n * jnp.dtype(dtype).itemsize


def _bf(x):
    return x if x.dtype == BF16 else x.astype(BF16)


def _mm(a, b):
    return jnp.dot(_bf(a), _bf(b), preferred_element_type=F32)


def _mm_nt(a, b):
    return lax.dot_general(_bf(a), _bf(b), (((1,), (1,)), ((), ())), preferred_element_type=F32)


def _mm_tn(a, b):
    return lax.dot_general(_bf(a), _bf(b), (((0,), (0,)), ((), ())), preferred_element_type=F32)


def _split2(x):
    hi = x.astype(BF16)
    lo = (x - hi.astype(F32)).astype(BF16)
    return hi, lo


def _mm3(a, b):
    ah, al = _split2(a)
    bh, bl = _split2(b)
    return _mm(ah, bh) + (_mm(ah, bl) + _mm(al, bh))


def _mm_exact_lhs(a_bf, b):
    b0, b1 = _split2(b)
    return _mm(a_bf, b0) + _mm(a_bf, b1)


def _head_ones(n, head):
    ri = lax.broadcasted_iota(jnp.int32, (n, n), 0) // head
    ci = lax.broadcasted_iota(jnp.int32, (n, n), 1) // head
    return jnp.where(ri == ci, 1.0, 0.0).astype(BF16)


def _rmsnorm(x, g):
    return x * lax.rsqrt(jnp.mean(x * x, axis=-1, keepdims=True) + EPS) * g


def _softplus(x):
    return jnp.maximum(x, 0.0) + jnp.log(1.0 + jnp.exp(-jnp.abs(x)))


def _mod_kernel(c_ref, w_ref, b_ref, o_ref):
    c = c_ref[...]
    o_ref[...] = _mm3(c * jax.nn.sigmoid(c), w_ref[...]) + b_ref[...]


def _adaln_mod(c_all, w_ada, b_ada):
    rows, n = c_all.shape[0], w_ada.shape[1]
    tn = 1536
    blocks = _nbytes((rows, D_MODEL), F32) + _nbytes((D_MODEL, tn), F32) + _nbytes((rows + 1, tn), F32)
    return pl.pallas_call(
        _mod_kernel,
        grid=(n // tn,),
        in_specs=[
            pl.BlockSpec((rows, D_MODEL), lambda j: (0, 0)),
            pl.BlockSpec((D_MODEL, tn), lambda j: (0, j)),
            pl.BlockSpec((1, tn), lambda j: (0, j)),
        ],
        out_specs=pl.BlockSpec((rows, tn), lambda j: (0, j)),
        out_shape=jax.ShapeDtypeStruct((rows, n), F32),
        compiler_params=pltpu.CompilerParams(
            dimension_semantics=("arbitrary",),
            vmem_limit_bytes=_vmem_limit(blocks, 3 * _nbytes((D_MODEL, tn), F32))),
        name="adaln_mod",
    )(c_all, w_ada, b_ada.reshape(1, n))


def _inproj_kernel(x_ref, xp_ref, xn_ref, mod_ref, nw_ref, mu_ref, w_ref, pm_ref, xs_ref, pg_ref):
    i = pl.program_id(1)
    tm = x_ref.shape[1]
    sh = mod_ref[0, :, 0:D_MODEL]
    sc = mod_ref[0, :, D_MODEL:2 * D_MODEL]
    x_ext = jnp.concatenate([xp_ref[0], x_ref[0], xn_ref[0]], 0)
    h_f32 = _rmsnorm(x_ext, nw_ref[...]) * (1.0 + sc) + sh
    h_ext = h_f32.astype(BF16)
    h = h_f32[SUBLANES:SUBLANES + tm].astype(BF16)
    a, b = MLA_IN_PAD, MLA_IN_PAD + RWKV_IN
    pm_ref[0] = _mm(h, w_ref[:, 0:a]).astype(BF16)
    pg_ref[0] = _mm(h, w_ref[:, b:IN_COLS_PAD]).astype(BF16)
    p = _mm(h_ext, w_ref[:, a:b])
    rid = lax.broadcasted_iota(jnp.int32, (tm + 2 * SUBLANES, 1), 0)
    outside = ((rid < SUBLANES) & (i == 0)) | ((rid >= tm + SUBLANES) & (i == pl.num_programs(1) - 1))
    p = jnp.where(outside, 0.0, p)
    n = tm + 2 * SUBLANES
    shifted = 0.5 * (pltpu.roll(p, 1, 0) + pltpu.roll(p, n - 1, 0))
    xs = p + mu_ref[...] * (shifted - p)
    xs_ref[0] = xs[SUBLANES:SUBLANES + tm]


def _in_proj(x, mod3, norm_w, mu, w_in_p, tm):
    bsz, t, _ = x.shape
    per = tm // SUBLANES
    last = t // SUBLANES - 1
    blocks = (_nbytes((tm + 2 * SUBLANES, D_MODEL), F32) + _nbytes((D_MODEL, IN_COLS_PAD), BF16)
              + _nbytes((tm, MLA_IN_PAD), BF16) + _nbytes((tm, RWKV_IN), F32) + _nbytes((tm, GATE_IN), BF16))
    return pl.pallas_call(
        _inproj_kernel,
        grid=(bsz, t // tm),
        in_specs=[
            pl.BlockSpec((1, tm, D_MODEL), lambda b, i: (b, i, 0)),
            pl.BlockSpec((1, SUBLANES, D_MODEL), lambda b, i: (b, jnp.maximum(i * per - 1, 0), 0)),
            pl.BlockSpec((1, SUBLANES, D_MODEL), lambda b, i: (b, jnp.minimum((i + 1) * per, last), 0)),
            pl.BlockSpec((1, 1, 6 * D_MODEL), lambda b, i: (b, 0, 0)),
            pl.BlockSpec((1, D_MODEL), lambda b, i: (0, 0)),
            pl.BlockSpec((1, RWKV_IN), lambda b, i: (0, 0)),
            pl.BlockSpec((D_MODEL, IN_COLS_PAD), lambda b, i: (0, 0)),
        ],
        out_specs=[
            pl.BlockSpec((1, tm, MLA_IN_PAD), lambda b, i: (b, i, 0)),
            pl.BlockSpec((1, tm, RWKV_IN), lambda b, i: (b, i, 0)),
            pl.BlockSpec((1, tm, GATE_IN), lambda b, i: (b, i, 0)),
        ],
        out_shape=[
            jax.ShapeDtypeStruct((bsz, t, MLA_IN_PAD), BF16),
            jax.ShapeDtypeStruct((bsz, t, RWKV_IN), F32),
            jax.ShapeDtypeStruct((bsz, t, GATE_IN), BF16),
        ],
        compiler_params=pltpu.CompilerParams(
            dimension_semantics=("parallel", "parallel"),
            vmem_limit_bytes=_vmem_limit(blocks, 2 * _nbytes((tm, IN_COLS_PAD), F32))),
        name="in_proj",
    )(x, x, x, mod3, norm_w, mu, w_in_p)


def _mla_prep_kernel(pm_ref, qn_ref, kvn_ref, wq_ref, wk_ref, wvt_ref, rc_ref, rs1_ref, rs2_ref,
                     q_ref, k_ref, vt_ref, *, tk):
    p = pm_ref[0].astype(F32)
    cq = _rmsnorm(p[:, 0:Q_LORA], qn_ref[...]).astype(BF16)
    ckv = _rmsnorm(p[:, Q_LORA:Q_LORA + KV_LORA], kvn_ref[...]).astype(BF16)
    rc, rs1, rs2 = rc_ref[...], rs1_ref[...], rs2_ref[...]

    def rope(x):
        return x * rc + pltpu.roll(x, QK_ROPE // 2, 1) * rs1 + pltpu.roll(x, LANES - QK_ROPE // 2, 1) * rs2

    k_pe = rope(p[:, Q_LORA + KV_LORA:MLA_IN_PAD])
    q = _mm(cq, wq_ref[...])
    kn = _mm(ckv, wk_ref[...])
    vt = _mm_nt(wvt_ref[...], ckv)
    scale = (QK_NOPE + QK_ROPE) ** -0.5 * LOG2E
    tail = jnp.where(lax.broadcasted_iota(jnp.int32, (V_ROWS - V_HEAD, tk), 0) == 0, 1.0, 0.0)
    for h in range(MLA_HEADS):
        sl = slice(h * HEAD_PAD, (h + 1) * HEAD_PAD)
        q_ref[0, h] = (rope(q[:, sl]) * scale).astype(BF16)
        k_ref[0, h] = (kn[:, sl] + k_pe).astype(BF16)
        for c in range(vt.shape[1] // tk):
            vt_ref[0, h, c] = jnp.concatenate(
                [vt[h * V_HEAD:(h + 1) * V_HEAD, c * tk:(c + 1) * tk], tail], 0).astype(BF16)


def _mla_prep(p_mla, q_norm, kv_norm, wq_p, wk_p, wvt, rope_c, rope_s1, rope_s2, tm, tk):
    bsz, t, _ = p_mla.shape
    assert tm % tk == 0
    hp = MLA_HEADS * HEAD_PAD
    blocks = (_nbytes((tm, MLA_IN_PAD), BF16) + _nbytes((Q_LORA + KV_LORA, hp), BF16)
              + _nbytes((MLA_WIDTH, KV_LORA), BF16) + 3 * _nbytes((tm, LANES), F32)
              + 2 * _nbytes((tm, hp), BF16) + _nbytes((MLA_WIDTH, tm), BF16))
    const = lambda b, i: (0, 0)
    return pl.pallas_call(
        functools.partial(_mla_prep_kernel, tk=tk),
        grid=(bsz, t // tm),
        in_specs=[
            pl.BlockSpec((1, tm, MLA_IN_PAD), lambda b, i: (b, i, 0)),
            pl.BlockSpec((1, Q_LORA), const),
            pl.BlockSpec((1, KV_LORA), const),
            pl.BlockSpec((Q_LORA, hp), const),
            pl.BlockSpec((KV_LORA, hp), const),
            pl.BlockSpec((MLA_WIDTH, KV_LORA), const),
            pl.BlockSpec((tm, LANES), lambda b, i: (i, 0)),
            pl.BlockSpec((tm, LANES), lambda b, i: (i, 0)),
            pl.BlockSpec((tm, LANES), lambda b, i: (i, 0)),
        ],
        out_specs=[
            pl.BlockSpec((1, MLA_HEADS, tm, HEAD_PAD), lambda b, i: (b, 0, i, 0)),
            pl.BlockSpec((1, MLA_HEADS, tm, HEAD_PAD), lambda b, i: (b, 0, i, 0)),
            pl.BlockSpec((1, MLA_HEADS, tm // tk, V_ROWS, tk), lambda b, i: (b, 0, i, 0, 0)),
        ],
        out_shape=[
            jax.ShapeDtypeStruct((bsz, MLA_HEADS, t, HEAD_PAD), BF16),
            jax.ShapeDtypeStruct((bsz, MLA_HEADS, t, HEAD_PAD), BF16),
            jax.ShapeDtypeStruct((bsz, MLA_HEADS, t // tk, V_ROWS, tk), BF16),
        ],
        compiler_params=pltpu.CompilerParams(
            dimension_semantics=("parallel", "parallel"),
            vmem_limit_bytes=_vmem_limit(blocks, 6 * _nbytes((tm, hp), F32))),
        name="mla_prep",
    )(p_mla, q_norm, kv_norm, wq_p, wk_p, wvt, rope_c, rope_s1, rope_s2)


def _attn_kernel(q_ref, k_ref, vt_ref, o_ref):
    q = q_ref[0, 0]
    tq = q.shape[0]
    nk, tk = vt_ref.shape[2], vt_ref.shape[4]

    def body(j, carry):
        m, acc = carry
        r0 = pl.multiple_of(j * tk, tk)
        st = _mm_nt(k_ref[0, 0, pl.ds(r0, tk), :], q)
        m_new = jnp.maximum(m, jnp.max(st, axis=0, keepdims=True))
        p = jnp.exp2(st - m_new)
        acc = jnp.exp2(m - m_new) * acc + _mm(vt_ref[0, 0, j], p)
        return m_new, acc

    init = (jnp.full((1, tq), -jnp.inf, F32), jnp.zeros((V_ROWS, tq), F32))
    _, acc = lax.fori_loop(0, nk, body, init, unroll=min(nk, 8))
    o_ref[0] = (acc[0:V_HEAD] / acc[V_HEAD:V_HEAD + 1]).astype(BF16)


def _attention(q, k, vt, tq):
    bsz, nh, t, _ = q.shape
    nk, tk = vt.shape[2], vt.shape[4]
    blocks = (_nbytes((tq, HEAD_PAD), BF16) + _nbytes((t, HEAD_PAD), BF16) + _nbytes((V_ROWS, t), BF16)
              + _nbytes((V_HEAD, tq), BF16))
    return pl.pallas_call(
        _attn_kernel,
        grid=(bsz, nh, t // tq),
        in_specs=[
            pl.BlockSpec((1, 1, tq, HEAD_PAD), lambda b, h, i: (b, h, i, 0)),
            pl.BlockSpec((1, 1, t, HEAD_PAD), lambda b, h, i: (b, h, 0, 0)),
            pl.BlockSpec((1, 1, nk, V_ROWS, tk), lambda b, h, i: (b, h, 0, 0, 0)),
        ],
        out_specs=pl.BlockSpec((1, V_HEAD, tq), lambda b, h, i: (b, h, i)),
        out_shape=jax.ShapeDtypeStruct((bsz, nh * V_HEAD, t), BF16),
        compiler_params=pltpu.CompilerParams(
            dimension_semantics=("parallel", "parallel", "arbitrary"),
            vmem_limit_bytes=_vmem_limit(blocks, 6 * _nbytes((tk, tq), F32))),
        name="mla_attn",
    )(q, k, vt)


def _rwkv_scan_kernel(xs_ref, wl_ref, w0_ref, a0_ref, kk_ref, ka_ref, y_ref,
                      r_sc, v_sc, kn_sc, lw_sc, a_sc, kd_sc, h_sc, lhs_sc, yl_sc, nt_sc, gam_sc, *, rows):
    d = pl.program_id(0)
    i = pl.program_id(2)
    c = CHUNK
    w = RWKV_WIDTH

    @pl.when(i == 0)
    def _():
        h_sc[...] = jnp.zeros_like(h_sc)

    xs = xs_ref[0]
    k = xs[:, w:2 * w]
    z = xs[:, LORA_OFF:LORA_OFF + LANES]
    lane = lax.broadcasted_iota(jnp.int32, z.shape, 1)
    pre = _mm(jnp.where(lane < DECAY_LORA, jnp.tanh(z), z), wl_ref[0])
    wlog = -_softplus(-(w0_ref[0] + pre[:, 0:w])) - 0.5
    a = jax.nn.sigmoid(a0_ref[0] + pre[:, w:2 * w])
    kn = k * kk_ref[...]
    ones_h = _head_ones(w, RWKV_HEAD)
    kn = kn * lax.rsqrt(_mm(kn * kn, ones_h) + 1e-12)
    r_sc[...] = xs[:, 0:w]
    v_sc[...] = xs[:, 2 * w:3 * w]
    kn_sc[...] = kn
    lw_sc[...] = -jnp.exp(wlog)
    a_sc[...] = a
    kd_sc[...] = k * (1.0 + (a - 1.0) * ka_ref[...])

    ri = lax.broadcasted_iota(jnp.int32, (2 * c, 2 * c), 0)
    ci = lax.broadcasted_iota(jnp.int32, (2 * c, 2 * c), 1)
    rt, cs = ri % c, ci % c
    before = (rt - cs) * (1 - 2 * d) > 0
    keep = before | ((ri >= c) & (cs == rt))
    rt_c = lax.broadcasted_iota(jnp.int32, (c, c), 0)
    cs_c = lax.broadcasted_iota(jnp.int32, (c, c), 1)
    tri = jnp.where((rt_c - cs_c) * (1 - 2 * d) >= 0, 1.0, 0.0).astype(BF16)
    eye = ri == ci
    lane_c = lax.broadcasted_iota(jnp.int32, (c, LANES), 1)
    lo_half = lane_c < RWKV_HEAD
    zeros_c = jnp.zeros((c, LANES), F32)
    nsub = rows // c

    def head_lo(x, h):
        s = x[:, (h // 2) * LANES:(h // 2 + 1) * LANES]
        if h % 2:
            s = pltpu.roll(s, RWKV_HEAD, 1)
        return jnp.where(lo_half[0:x.shape[0]], s, 0.0)

    def head_hi(x, h):
        s = x[:, (h // 2) * LANES:(h // 2 + 1) * LANES]
        if h % 2 == 0:
            s = pltpu.roll(s, RWKV_HEAD, 1)
        return jnp.where(lo_half[0:x.shape[0]], 0.0, s)

    chains = [(s, h) for s in range(nsub) for h in range(RWKV_HEADS)]
    rh, vh, w_bot, pw, x, f, e_hat, g_hat = {}, {}, {}, {}, {}, {}, {}, {}
    for s in range(nsub):
        sl = slice(s * c, (s + 1) * c)
        lw = lw_sc[sl, :]
        cum = _mm_exact_lhs(tri, lw)
        tot = jnp.sum(lw, axis=0, keepdims=True)
        kn_c, a_c, kd_c = kn_sc[sl, :], a_sc[sl, :], kd_sc[sl, :]
        e_neg = jnp.exp(-cum)
        e_end = jnp.exp(tot - cum)
        r_t = r_sc[sl, :] * jnp.exp(cum)
        a_t = -kn_c * jnp.exp(cum - lw)
        b_t = kn_c * a_c * e_neg
        k_t = kd_c * e_neg
        b_e = kn_c * a_c * e_end
        k_e = kd_c * e_end
        g_end = jnp.exp(tot)
        v_c = v_sc[sl, :]
        for h in range(RWKV_HEADS):
            a_lo = head_lo(a_t, h)
            rh[s, h], vh[s, h] = head_lo(r_t, h), head_hi(v_c, h)
            g = _mm_nt(jnp.concatenate([a_lo, rh[s, h]], 0),
                       jnp.concatenate([head_lo(b_t, h), head_lo(k_t, h)], 0))
            g = jnp.where(keep, g, 0.0)
            w_bot[s, h] = g[c:2 * c]
            pw[s, h] = jnp.where(lo_half, g[0:c], 0.0)
            x[s, h] = (a_lo, jnp.where(lo_half, 0.0, g[0:c]))
            g_hat[s, h] = head_lo(g_end, h)
        for j in range(RWKV_HEADS // 2):
            e_hat[s, j] = jnp.concatenate([b_e[:, j * LANES:(j + 1) * LANES], k_e[:, j * LANES:(j + 1) * LANES]], 0)
    for ch in chains:
        a_lo, l_ak = x[ch]
        x[ch] = a_lo + _mm(l_ak, jnp.concatenate([zeros_c, vh[ch]], 0))
    zeros_xp = jnp.zeros((c, 2 * LANES), F32)
    for step in range(6):
        for ch in chains:
            if step < 5:
                z = _mm(pw[ch], jnp.concatenate([jnp.concatenate([x[ch], pw[ch]], 1), zeros_xp], 0))
                x[ch] = x[ch] + z[:, 0:LANES]
                pw[ch] = z[:, LANES:2 * LANES]
            else:
                x[ch] = x[ch] + _mm(pw[ch], jnp.concatenate([x[ch], zeros_c], 0))
    for ch in chains:
        f[ch] = jnp.concatenate([x[ch], vh[ch]], 0)
        wf = _mm(w_bot[ch], f[ch])
        lhs_sc[ch[0], ch[1], c:2 * c] = jnp.where(lo_half, rh[ch] + wf, 0.0).astype(BF16)
        yl_sc[ch[0], ch[1]] = jnp.where(lo_half, 0.0, wf)
        gcol = jnp.sum(jnp.where(eye, g_hat[ch], 0.0), axis=1, keepdims=True)[0:c]
        gam_sc[ch[0], ch[1]] = jnp.broadcast_to(gcol, (c, LANES))
    for s in range(nsub):
        for j in range(RWKV_HEADS // 2):
            mn2 = _mm_tn(e_hat[s, j], jnp.concatenate([f[s, 2 * j], f[s, 2 * j + 1]], 1))
            for h, mn in ((2 * j, mn2[0:c, 0:LANES]), (2 * j + 1, mn2[c:2 * c, LANES:2 * LANES])):
                lhs_sc[s, h, 0:c] = jnp.where(lo_half, mn, 0.0).astype(BF16)
                nt_sc[s, h] = jnp.where(lo_half, 0.0, mn)

    def advance(s, carry):
        sub = s + d * (nsub - 1 - 2 * s)
        sl = pl.ds(pl.multiple_of(sub * c, c), c)
        for j in range(RWKV_HEADS // 2):
            ys = []
            for h in (2 * j, 2 * j + 1):
                hs = h_sc[h]
                res = _mm(lhs_sc[sub, h], hs)
                h_new = gam_sc[sub, h] * hs[0:c] + res[0:c] + nt_sc[sub, h]
                h_sc[h] = jnp.concatenate([h_new, zeros_c], 0)
                ys.append(res[c:2 * c] + yl_sc[sub, h])
            y_ref[0, 0, sl, j * LANES:(j + 1) * LANES] = pltpu.roll(ys[0], RWKV_HEAD, 1) + ys[1]
        return carry

    lax.fori_loop(0, nsub, advance, 0)


def _rwkv_scan(xs, w_lora, w0, a0, k_k, k_a, rows):
    bsz, t, _ = xs.shape
    nblk = t // rows
    w = RWKV_WIDTH
    blk_of = lambda d, b, i: i + d * (nblk - 1 - 2 * i)
    vec = lambda n: pl.BlockSpec((1, n), lambda d, b, i: (0, 0))
    dvec = lambda n: pl.BlockSpec((1, 1, n), lambda d, b, i: (d, 0, 0))
    blocks = (_nbytes((rows, RWKV_IN), F32) + _nbytes((LANES, 2 * w), F32) + _nbytes((rows, w), F32))
    per_chain = (rows // CHUNK, RWKV_HEADS, CHUNK, LANES)
    lhs_rows = (rows // CHUNK, RWKV_HEADS, 2 * CHUNK, LANES)
    scratch = (6 * _nbytes((rows, w), F32) + _nbytes((RWKV_HEADS, LANES, LANES), F32)
               + 3 * _nbytes(per_chain, F32) + _nbytes(lhs_rows, BF16))
    return pl.pallas_call(
        functools.partial(_rwkv_scan_kernel, rows=rows),
        grid=(2, bsz, nblk),
        in_specs=[
            pl.BlockSpec((1, rows, RWKV_IN), lambda d, b, i: (b, blk_of(d, b, i), 0)),
            pl.BlockSpec((1, LANES, 2 * w), lambda d, b, i: (d, 0, 0)),
            dvec(w), dvec(w), vec(w), vec(w),
        ],
        out_specs=pl.BlockSpec((1, 1, rows, w), lambda d, b, i: (d, b, blk_of(d, b, i), 0)),
        out_shape=jax.ShapeDtypeStruct((2, bsz, t, w), F32),
        scratch_shapes=([pltpu.VMEM((rows, w), F32)] * 6 + [pltpu.VMEM((RWKV_HEADS, LANES, LANES), F32)]
                        + [pltpu.VMEM(lhs_rows, BF16)] + [pltpu.VMEM(per_chain, F32)] * 3),
        compiler_params=pltpu.CompilerParams(
            dimension_semantics=("parallel", "parallel", "arbitrary"),
            vmem_limit_bytes=_vmem_limit(blocks, scratch + 8 * _nbytes((rows, RWKV_IN), F32))),
        name="rwkv_scan",
    )(xs, w_lora, w0, a0, k_k, k_a)


def _mix_kernel(x_ref, mod_ref, ot_ref, y_ref, xs_ref, pg_ref, wi_ref, a0_ref, ka_ref,
                rk_ref, wg_ref, lw_ref, lb_ref, wmo_ref, wro_ref, wo_ref, o_ref):
    w = RWKV_WIDTH
    xs = xs_ref[0]
    r, k, v = xs[:, 0:w], xs[:, w:2 * w], xs[:, 2 * w:3 * w]
    pre = _mm(xs[:, LORA_OFF:LORA_OFF + LANES], wi_ref[...])
    ka = ka_ref[...]
    bonus_k = (k * (1.0 + (jax.nn.sigmoid(a0_ref[0] + pre[:, 0:w]) - 1.0) * ka)
               + k * (1.0 + (jax.nn.sigmoid(a0_ref[1] + pre[:, w:2 * w]) - 1.0) * ka))
    gate = _mm(jax.nn.sigmoid(xs[:, GATE_OFF:GATE_OFF + GATE_LORA]), wg_ref[...])
    ones_h = _head_ones(w, RWKV_HEAD)
    inv_n = 1.0 / RWKV_HEAD
    y = y_ref[0, 0] + y_ref[1, 0]
    mean = _mm(y, ones_h) * inv_n
    yc = y - mean
    var = _mm(yc * yc, ones_h) * inv_n
    yn = yc * lax.rsqrt(var + LNX_EPS) * lw_ref[...] + lb_ref[...]
    bonus = _mm(r * bonus_k * rk_ref[...], ones_h) * v
    o_rwkv = _mm((yn + bonus) * gate, wro_ref[...])
    o_mla = _mm_tn(ot_ref[0], wmo_ref[...])
    pg = pg_ref[0].astype(F32)
    merged = jax.nn.sigmoid(pg[:, 0:D_MODEL]) * o_mla + jax.nn.sigmoid(pg[:, D_MODEL:2 * D_MODEL]) * o_rwkv
    g1 = mod_ref[0, :, 2 * D_MODEL:3 * D_MODEL]
    o_ref[0] = x_ref[0] + g1 * _mm(merged, wo_ref[...])


def _mix_out(x, mod3, o_t, y, xs, p_gate, w_iclr, a0, k_a, r_k, w_gate, lnx_w, lnx_b,
             w_mla_o, w_rwkv_o, w_out, tm):
    bsz, t, _ = x.shape
    w = RWKV_WIDTH
    const2 = lambda b, i: (0, 0)
    vec = lambda n: pl.BlockSpec((1, n), const2)
    blocks = (2 * _nbytes((tm, D_MODEL), F32) + _nbytes((MLA_WIDTH, tm), BF16) + 2 * _nbytes((tm, w), F32)
              + _nbytes((tm, RWKV_IN), F32) + _nbytes((tm, GATE_IN), BF16)
              + _nbytes((LANES, 2 * w), F32) + _nbytes((GATE_LORA, w), BF16)
              + 2 * _nbytes((w, D_MODEL), BF16) + _nbytes((D_MODEL, D_MODEL), BF16))
    return pl.pallas_call(
        _mix_kernel,
        grid=(bsz, t // tm),
        in_specs=[
            pl.BlockSpec((1, tm, D_MODEL), lambda b, i: (b, i, 0)),
            pl.BlockSpec((1, 1, 6 * D_MODEL), lambda b, i: (b, 0, 0)),
            pl.BlockSpec((1, MLA_WIDTH, tm), lambda b, i: (b, 0, i)),
            pl.BlockSpec((2, 1, tm, w), lambda b, i: (0, b, i, 0)),
            pl.BlockSpec((1, tm, RWKV_IN), lambda b, i: (b, i, 0)),
            pl.BlockSpec((1, tm, GATE_IN), lambda b, i: (b, i, 0)),
            pl.BlockSpec((LANES, 2 * w), const2),
            pl.BlockSpec((2, 1, w), lambda b, i: (0, 0, 0)),
            vec(w), vec(w),
            pl.BlockSpec((GATE_LORA, w), const2),
            vec(w), vec(w),
            pl.BlockSpec((MLA_WIDTH, D_MODEL), const2),
            pl.BlockSpec((w, D_MODEL), const2),
            pl.BlockSpec((D_MODEL, D_MODEL), const2),
        ],
        out_specs=pl.BlockSpec((1, tm, D_MODEL), lambda b, i: (b, i, 0)),
        out_shape=jax.ShapeDtypeStruct((bsz, t, D_MODEL), F32),
        compiler_params=pltpu.CompilerParams(
            dimension_semantics=("parallel", "parallel"),
            vmem_limit_bytes=_vmem_limit(blocks, 10 * _nbytes((tm, RWKV_IN), F32))),
        name="mix_out",
    )(x, mod3, o_t, y, xs, p_gate, w_iclr, a0, k_a, r_k, w_gate, lnx_w, lnx_b, w_mla_o, w_rwkv_o, w_out)


def _ffn_kernel(x_ref, mod_ref, nw_ref, wi_ref, wo_ref, fn_ref, o_ref, acc_ref, *, tf):
    x = x_ref[0]
    sh = mod_ref[0, :, 3 * D_MODEL:4 * D_MODEL]
    sc = mod_ref[0, :, 4 * D_MODEL:5 * D_MODEL]
    g2 = mod_ref[0, :, 5 * D_MODEL:6 * D_MODEL]
    h = (_rmsnorm(x, nw_ref[...]) * (1.0 + sc) + sh).astype(BF16)
    for j in range(D_FF // tf):
        u = _mm(h, wi_ref[:, j * tf:(j + 1) * tf])
        zg = _mm(h, wi_ref[:, D_FF + j * tf:D_FF + (j + 1) * tf])
        part = _mm(u * jax.nn.sigmoid(u) * zg, wo_ref[j * tf:(j + 1) * tf, :])
        if j == 0:
            acc_ref[...] = part
        else:
            acc_ref[...] += part
    o_ref[0] = _rmsnorm(x + g2 * acc_ref[...], fn_ref[...])


def _ffn(x, mod3, norm_w, w_in, w_out, final_w, tm, tf):
    bsz, t, _ = x.shape
    assert D_FF % tf == 0 and tf % LANES == 0
    blocks = (2 * _nbytes((tm, D_MODEL), F32) + _nbytes((D_MODEL, 2 * D_FF), BF16)
              + _nbytes((D_FF, D_MODEL), BF16))
    const2 = lambda b, i: (0, 0)
    return pl.pallas_call(
        functools.partial(_ffn_kernel, tf=tf),
        grid=(bsz, t // tm),
        in_specs=[
            pl.BlockSpec((1, tm, D_MODEL), lambda b, i: (b, i, 0)),
            pl.BlockSpec((1, 1, 6 * D_MODEL), lambda b, i: (b, 0, 0)),
            pl.BlockSpec((1, D_MODEL), const2),
            pl.BlockSpec((D_MODEL, 2 * D_FF), const2),
            pl.BlockSpec((D_FF, D_MODEL), const2),
            pl.BlockSpec((1, D_MODEL), const2),
        ],
        out_specs=pl.BlockSpec((1, tm, D_MODEL), lambda b, i: (b, i, 0)),
        out_shape=jax.ShapeDtypeStruct((bsz, t, D_MODEL), F32),
        scratch_shapes=[pltpu.VMEM((tm, D_MODEL), F32)],
        compiler_params=pltpu.CompilerParams(
            dimension_semantics=("parallel", "parallel"),
            vmem_limit_bytes=_vmem_limit(blocks, _nbytes((tm, D_MODEL), F32) + 6 * _nbytes((tm, tf), F32))),
        name="ffn",
    )(x, mod3, norm_w, w_in, w_out, final_w)


def _rope_tables(t):
    half = QK_ROPE // 2
    inv = ROPE_BASE ** (-jnp.arange(half, dtype=F32) / half)
    ang = jnp.arange(t, dtype=F32)[:, None] * inv[None, :]
    cos, sin = jnp.cos(ang), jnp.sin(ang)
    z = lambda n: jnp.zeros((t, n), F32)
    rc = jnp.concatenate([jnp.ones((t, QK_NOPE), F32), cos, cos, z(LANES - QK_NOPE - QK_ROPE)], 1)
    rs1 = jnp.concatenate([z(QK_NOPE + half), sin, z(LANES - QK_NOPE - QK_ROPE)], 1)
    rs2 = jnp.concatenate([z(QK_NOPE), -sin, z(LANES - QK_NOPE - half)], 1)
    return rc, rs1, rs2


def _prepare_params(w_in, w_uq, w_ukv, w_decay_up, w_iclr_up, w_gate_up, w_mla_o, w_rwkv_o, w_out,
                    w_ffn_in, w_ffn_out):
    dqk = QK_NOPE + QK_ROPE
    kpe_tile = jnp.zeros((D_MODEL, LANES), F32).at[:, QK_NOPE:QK_NOPE + QK_ROPE].set(
        w_in[:, Q_LORA + KV_LORA:MLA_IN])
    w_in_p = jnp.concatenate([w_in[:, 0:Q_LORA + KV_LORA], kpe_tile, w_in[:, MLA_IN:]], 1).astype(BF16)
    wq = w_uq.reshape(Q_LORA, MLA_HEADS, dqk)
    wq_p = jnp.pad(wq, ((0, 0), (0, 0), (0, HEAD_PAD - dqk))).reshape(Q_LORA, MLA_HEADS * HEAD_PAD).astype(BF16)
    wkv = w_ukv.reshape(KV_LORA, MLA_HEADS, QK_NOPE + V_HEAD)
    wk_p = jnp.pad(wkv[:, :, 0:QK_NOPE], ((0, 0), (0, 0), (0, HEAD_PAD - QK_NOPE))).reshape(
        KV_LORA, MLA_HEADS * HEAD_PAD).astype(BF16)
    wvt = wkv[:, :, QK_NOPE:].reshape(KV_LORA, MLA_WIDTH).T.astype(BF16)
    zl = jnp.zeros((DECAY_LORA, RWKV_WIDTH), F32)
    w_lora = jnp.stack([jnp.concatenate([jnp.concatenate([w_decay_up[d], zl], 1),
                                         jnp.concatenate([zl, w_iclr_up[d]], 1)], 0) for d in range(2)])
    w_iclr = jnp.concatenate([jnp.zeros((DECAY_LORA, 2 * RWKV_WIDTH), F32),
                              jnp.concatenate([w_iclr_up[0], w_iclr_up[1]], 1)], 0)
    return dict(w_in_p=w_in_p, wq_p=wq_p, wk_p=wk_p, wvt=wvt, w_lora=w_lora, w_iclr=w_iclr,
                w_gate=w_gate_up.astype(BF16), w_mla_o=w_mla_o.astype(BF16), w_rwkv_o=w_rwkv_o.astype(BF16),
                w_out=w_out.astype(BF16), w_ffn_in=w_ffn_in.astype(BF16), w_ffn_out=w_ffn_out.astype(BF16))


def _tiles(t):
    return dict(tm_proj=min(512, t), tm_prep=min(1024, t), tq=min(2048, t), tk=min(1024, t),
                rows_scan=min(512, t), tm_mix=min(512, t), tm_ffn=min(512, t), tf=256)


def _encoder(x, mod, pp, norm_mix, q_a_norm, kv_a_norm, mu_shift, w0, a0, k_k, k_a, r_k, lnx_w, lnx_b,
             norm_ffn, final_norm):
    bsz, t, _ = x.shape
    ts = _tiles(t)
    row = lambda v: v.reshape(1, -1)
    mod3 = mod.reshape(bsz, 1, 6 * D_MODEL)
    p_mla, xs, p_gate = _in_proj(x, mod3, row(norm_mix), row(mu_shift), pp["w_in_p"], ts["tm_proj"])
    q, k, vt = _mla_prep(p_mla, row(q_a_norm), row(kv_a_norm), pp["wq_p"], pp["wk_p"], pp["wvt"],
                         *_rope_tables(t), ts["tm_prep"], ts["tk"])
    o_t = _attention(q, k, vt, ts["tq"])
    y = _rwkv_scan(xs, pp["w_lora"], w0.reshape(2, 1, -1), a0.reshape(2, 1, -1), row(k_k), row(k_a),
                   ts["rows_scan"])
    x1 = _mix_out(x, mod3, o_t, y, xs, p_gate, pp["w_iclr"], a0.reshape(2, 1, -1), row(k_a),
                  row(r_k), pp["w_gate"], row(lnx_w), row(lnx_b), pp["w_mla_o"], pp["w_rwkv_o"], pp["w_out"],
                  ts["tm_mix"])
    return _ffn(x1, mod3, row(norm_ffn), pp["w_ffn_in"], pp["w_ffn_out"], row(final_norm), ts["tm_ffn"], ts["tf"])


def kernel(x_prompt, x_sample, c_prompt, c_sample, w_ada, b_ada, norm_mix, w_in, q_a_norm, kv_a_norm, w_uq, w_ukv, mu_shift, w0, w_decay_up, a0, w_iclr_up, w_gate_up, k_k, k_a, r_k, lnx_w, lnx_b, w_mla_o, w_rwkv_o, w_out, norm_ffn, w_ffn_in, w_ffn_out, final_norm):
    pp = _prepare_params(w_in[0], w_uq[0], w_ukv[0], w_decay_up[0], w_iclr_up[0], w_gate_up[0], w_mla_o[0],
                         w_rwkv_o[0], w_out[0], w_ffn_in[0], w_ffn_out[0])
    nb = x_prompt.shape[0]
    mod = _adaln_mod(jnp.concatenate([c_prompt, c_sample], 0), w_ada[0], b_ada[0])
    args = (pp, norm_mix[0], q_a_norm[0], kv_a_norm[0], mu_shift[0], w0[0], a0[0], k_k[0], k_a[0], r_k[0],
            lnx_w[0], lnx_b[0], norm_ffn[0], final_norm)
    return (_encoder(x_prompt, mod[:nb], *args), _encoder(x_sample, mod[nb:], *args))
```
